```python
import math
import jax, jax.numpy as jnp
from jax import lax
import numpy as np

D_MODEL = 2048
BATCH = 4
SEQ = 2048
DEPTH = 1

RET_HEADS = 8
RET_QK_DIM = 128
RET_V_DIM = 256
RET_CHUNK = 128
MOBA_HEADS = 16
MOBA_HEAD_DIM = 128
MOBA_BLOCK = 256
MOBA_TOPK = 3
MOBA_Q_CHUNK = 16
ROPE_THETA = 10000.0
NEG_INF = -1e30
N_EXPERTS = 32
TOP_K = 4
D_FF = D_MODEL
SWIGLU_LIMIT = 7.0
SWIGLU_ALPHA = 1.702
MOE_ROW_BLOCK = 256
RMS_EPS = 1e-5

RET_QK_W = RET_HEADS * RET_QK_DIM
RET_V_W = RET_HEADS * RET_V_DIM
MOBA_W = MOBA_HEADS * MOBA_HEAD_DIM
N_BRANCHES = 2
IN_SPLITS = (RET_QK_W, RET_QK_W, RET_V_W, RET_V_W, MOBA_W, MOBA_W, MOBA_W, N_BRANCHES * D_MODEL)
IN_WIDTH = sum(IN_SPLITS)

kernel_name = "hybrid_retention_moba_gated_moe"


def rms_norm(x, g):
    xf = x.astype(jnp.float32)
    y = xf * lax.rsqrt(jnp.mean(xf * xf, axis=-1, keepdims=True) + RMS_EPS)
    return (y * g.astype(jnp.float32)).astype(x.dtype)


def split_points(sizes):
    pts, acc = [], 0
    for s in sizes[:-1]:
        acc += s
        pts.append(acc)
    return pts


def rotary(x, positions):
    half = x.shape[-1] // 2
    inv_freq = ROPE_THETA ** (-jnp.arange(half, dtype=jnp.float32) / half)
    ang = positions.astype(jnp.float32)[..., None] * inv_freq
    cos = jnp.cos(ang)[:, :, None, :]
    sin = jnp.sin(ang)[:, :, None, :]
    xf = x.astype(jnp.float32)
    x1, x2 = xf[..., :half], xf[..., half:]
    out = jnp.concatenate([x1 * cos - x2 * sin, x2 * cos + x1 * sin], axis=-1)
    return out.astype(x.dtype)


def retention(q, k, v):
    B, S, H, dk = q.shape
    dv = v.shape[-1]
    C = RET_CHUNK
    N = S // C
    log_gamma = jnp.log1p(-jnp.exp2(-5.0 - jnp.arange(H, dtype=jnp.float32)))
    pos = jnp.arange(C, dtype=jnp.float32)
    qc = q.astype(jnp.float32).reshape(B, N, C, H, dk)
    kc = k.astype(jnp.float32).reshape(B, N, C, H, dk)
    vc = v.astype(jnp.float32).reshape(B, N, C, H, dv)
    diff = pos[:, None] - pos[None, :]
    intra_decay = jnp.where(diff[None] >= 0.0,
                            jnp.exp(jnp.maximum(diff, 0.0)[None] * log_gamma[:, None, None]), 0.0)
    scores = jnp.einsum('bnihd,bnjhd->bnhij', qc, kc) * intra_decay
    o_intra = jnp.einsum('bnhij,bnjhe->bnihe', scores, vc)
    k_w = jnp.exp((C - 1.0 - pos)[:, None] * log_gamma[None, :])
    inc = jnp.einsum('bnjhd,bnjhe->nbhde', kc * k_w[:, :, None], vc)
    chunk_decay = jnp.exp(C * log_gamma)[:, None, None]

    def step(state, inc_n):
        return chunk_decay * state + inc_n, state

    _, prev = lax.scan(step, jnp.zeros((B, H, dk, dv), jnp.float32), inc)
    q_w = jnp.exp((pos + 1.0)[:, None] * log_gamma[None, :])
    o_cross = jnp.einsum('bnihd,nbhde->bnihe', qc * q_w[:, :, None], prev)
    return (o_intra + o_cross).reshape(B, S, H, dv)


def moba_attention(q, k, v):
    B, S, H, Dh = q.shape
    L = MOBA_BLOCK
    QC = MOBA_Q_CHUNK
    n_kb = -(-S // L)
    s_pad = n_kb * L
    pad = ((0, 0), (0, s_pad - S), (0, 0), (0, 0))
    qh = q.transpose(0, 2, 1, 3)
    kh = jnp.pad(k, pad).transpose(0, 2, 1, 3)
    vh = jnp.pad(v, pad).transpose(0, 2, 1, 3)
    kb = kh.reshape(B, H, n_kb, L, Dh)
    vb = vh.reshape(B, H, n_kb, L, Dh)
    k_mean = jnp.mean(kb.astype(jnp.float32), axis=3)
    n_sel = min(MOBA_TOPK, n_kb)
    scale = Dh ** -0.5
    bi = jnp.arange(B)[:, None, None, None]
    hi = jnp.arange(H)[None, :, None, None]
    blk_ids = jnp.arange(n_kb)
    key_off = jnp.arange(L)
    q_off = jnp.arange(QC)

    def chunk(start):
        cur = start // L
        qc = lax.dynamic_slice_in_dim(qh, start, QC, axis=2)
        gate = jnp.einsum('bhqd,bhnd->bhqn', qc.astype(jnp.float32), k_mean)
        gate = jnp.where(blk_ids < cur, gate, NEG_INF)
        _, sel = lax.top_k(gate, n_sel)
        valid = sel < cur
        kg = kb[bi, hi, sel]
        vg = vb[bi, hi, sel]
        s_sel = jnp.einsum('bhqd,bhqnld->bhqnl', qc, kg).astype(jnp.float32) * scale
        s_sel = jnp.where(valid[..., None], s_sel, NEG_INF)
        k_own = lax.dynamic_slice_in_dim(kh, cur * L, L, axis=2)
        v_own = lax.dynamic_slice_in_dim(vh, cur * L, L, axis=2)
        s_own = jnp.einsum('bhqd,bhld->bhql', qc, k_own).astype(jnp.float32) * scale
        causal = (cur * L + key_off)[None, :] <= (start + q_off)[:, None]
        s_own = jnp.where(causal, s_own, NEG_INF)
        s_all = jnp.concatenate([s_sel.reshape(B, H, QC, n_sel * L), s_own], axis=-1)
        p = jax.nn.softmax(s_all, axis=-1).astype(v.dtype)
        p_sel = p[..., :n_sel * L].reshape(B, H, QC, n_sel, L)
        p_own = p[..., n_sel * L:]
        return (jnp.einsum('bhqnl,bhqnld->bhqd', p_sel, vg)
                + jnp.einsum('bhql,bhld->bhqd', p_own, v_own))

    starts = jnp.arange(S // QC) * QC
    out = lax.map(chunk, starts)
    return out.transpose(1, 0, 3, 2, 4).reshape(B, S, H * Dh)


def moe_ffn(h, router_w, router_b, w_gu, b_gu, w_down, b_down):
    B, S, D = h.shape
    T = B * S
    R = MOE_ROW_BLOCK
    ht = h.reshape(T, D)
    logits = (ht @ router_w + router_b).astype(jnp.float32)
    top_val, top_idx = lax.top_k(logits, TOP_K)
    comb = jax.nn.softmax(top_val, axis=-1)
    TK = T * TOP_K
    e_flat = top_idx.reshape(TK)
    order = jnp.argsort(e_flat)
    e_sorted = e_flat[order]
    tok_sorted = order // TOP_K
    w_sorted = comb.reshape(TK)[order]
    counts = jnp.zeros((N_EXPERTS,), jnp.int32).at[e_flat].add(1)
    starts = jnp.cumsum(counts) - counts
    padded = (counts + R - 1) // R * R
    pends = jnp.cumsum(padded)
    pstarts = pends - padded
    dest = pstarts[e_sorted] + (jnp.arange(TK) - starts[e_sorted])
    n_blocks = -(-TK // R) + N_EXPERTS
    P = n_blocks * R
    row_tok = jnp.zeros((P,), jnp.int32).at[dest].set(tok_sorted)
    row_w = jnp.zeros((P,), jnp.float32).at[dest].set(w_sorted)
    block_e = jnp.minimum(jnp.searchsorted(pends, jnp.arange(n_blocks) * R, side='right'),
                          N_EXPERTS - 1).astype(jnp.int32)
    x_rows = ht[row_tok].reshape(n_blocks, R, D)

    def expert_block(args):
        xb, e = args
        gu = xb @ w_gu[e] + b_gu[e]
        gate = jnp.minimum(gu[:, :D_FF], SWIGLU_LIMIT)
        up = jnp.clip(gu[:, D_FF:], -SWIGLU_LIMIT, SWIGLU_LIMIT)
        act = (up + 1.0) * gate * jax.nn.sigmoid(SWIGLU_ALPHA * gate)
        return act @ w_down[e] + b_down[e]

    y_rows = lax.map(expert_block, (x_rows, block_e)).reshape(P, D)
    y_rows = (y_rows * row_w[:, None]).astype(h.dtype)
    y = jnp.zeros((T, D), h.dtype).at[row_tok].add(y_rows)
    return y.reshape(B, S, D)


def setup_inputs(seed: int = 0) -> dict:
    key = jax.random.key(seed)
    ks = jax.random.split(key, 16)
    f32 = jnp.float32
    nrm = lambda k, shape, fan_in: jax.random.normal(k, shape, f32) * (fan_in ** -0.5)
    x = jax.random.normal(ks[0], (BATCH, SEQ, D_MODEL), f32)
    offset = jax.random.randint(ks[1], (BATCH, 1), 0, 4096, dtype=jnp.int32)
    positions = offset + jnp.arange(SEQ, dtype=jnp.int32)[None, :]
    return {
        "x": x,
        "positions": positions,
        "norm_mix_g": 1.0 + 0.01 * jax.random.normal(ks[2], (DEPTH, D_MODEL), f32),
        "w_in": nrm(ks[3], (DEPTH, D_MODEL, IN_WIDTH), D_MODEL),
        "ret_w_o": nrm(ks[4], (DEPTH, RET_V_W, D_MODEL), RET_V_W),
        "moba_w_o": nrm(ks[5], (DEPTH, MOBA_W, D_MODEL), MOBA_W),
        "w_out": nrm(ks[6], (DEPTH, D_MODEL, D_MODEL), D_MODEL),
        "norm_ffn_g": 1.0 + 0.01 * jax.random.normal(ks[7], (DEPTH, D_MODEL), f32),
        "router_w": nrm(ks[8], (DEPTH, D_MODEL, N_EXPERTS), D_MODEL),
        "router_b": 0.01 * jax.random.normal(ks[9], (DEPTH, N_EXPERTS), f32),
        "exp_w_gu": nrm(ks[10], (DEPTH, N_EXPERTS, D_MODEL, 2 * D_FF), D_MODEL),
        "exp_b_gu": 0.01 * jax.random.normal(ks[11], (DEPTH, N_EXPERTS, 2 * D_FF), f32),
        "exp_w_down": nrm(ks[12], (DEPTH, N_EXPERTS, D_FF, D_MODEL), D_FF),
        "exp_b_down": 0.01 * jax.random.normal(ks[13], (DEPTH, N_EXPERTS, D_MODEL), f32),
        "norm_final_g": 1.0 + 0.01 * jax.random.normal(ks[14], (D_MODEL,), f32),
    }


def reference(x, positions, norm_mix_g, w_in, ret_w_o, moba_w_o, w_out, norm_ffn_g,
              router_w, router_b, exp_w_gu, exp_b_gu, exp_w_down, exp_b_down, norm_final_g):
    B, S, D = x.shape
    for l in range(DEPTH):
        h = rms_norm(x, norm_mix_g[l])
        proj = jnp.einsum('bsd,df->bsf', h, w_in[l])
        rq, rk, rv, rg, mq, mk, mv, gates = jnp.split(proj, split_points(IN_SPLITS), axis=-1)
        rq = rotary(rq.reshape(B, S, RET_HEADS, RET_QK_DIM), positions)
        rk = rotary(rk.reshape(B, S, RET_HEADS, RET_QK_DIM), positions) * (RET_QK_DIM ** -0.5)
        o_ret = retention(rq, rk, rv.reshape(B, S, RET_HEADS, RET_V_DIM))
        o_ret = o_ret * lax.rsqrt(jnp.mean(o_ret * o_ret, axis=-1, keepdims=True) + RMS_EPS)
        o_ret = o_ret.astype(x.dtype) * jax.nn.silu(rg.reshape(B, S, RET_HEADS, RET_V_DIM))
        y_ret = o_ret.reshape(B, S, RET_V_W) @ ret_w_o[l]
        mq = rotary(mq.reshape(B, S, MOBA_HEADS, MOBA_HEAD_DIM), positions)
        mk = rotary(mk.reshape(B, S, MOBA_HEADS, MOBA_HEAD_DIM), positions)
        o_moba = moba_attention(mq, mk, mv.reshape(B, S, MOBA_HEADS, MOBA_HEAD_DIM))
        y_moba = o_moba @ moba_w_o[l]
        g = jax.nn.sigmoid(gates)
        merged = g[..., :D] * y_ret + g[..., D:] * y_moba
        x = x + merged @ w_out[l]
        h2 = rms_norm(x, norm_ffn_g[l])
        x = x + moe_ffn(h2, router_w[l], router_b[l], exp_w_gu[l], exp_b_gu[l],
                        exp_w_down[l], exp_b_down[l])
    return rms_norm(x, norm_final_g)
```

```python
import functools

import jax
import jax.numpy as jnp
from jax import lax
from jax.experimental import pallas as pl
from jax.experimental.pallas import tpu as pltpu

F32 = jnp.float32
BF16 = jnp.bfloat16

D_MODEL = 2048
RET_HEADS = 8
RET_QK_DIM = 128
RET_V_DIM = 256
RET_CHUNK = 128
MOBA_HEADS = 16
MOBA_HEAD_DIM = 128
MOBA_BLOCK = 256
MOBA_TOPK = 3
ROPE_THETA = 10000.0
NEG_INF = -1e30
N_EXPERTS = 32
TOP_K = 4
D_FF = D_MODEL
SWIGLU_LIMIT = 7.0
SWIGLU_ALPHA = 1.702
RMS_EPS = 1e-5

RET_QK_W = RET_HEADS * RET_QK_DIM
RET_V_W = RET_HEADS * RET_V_DIM
MOBA_W = MOBA_HEADS * MOBA_HEAD_DIM
IN_WIDTH = 2 * RET_QK_W + 2 * RET_V_W + 3 * MOBA_W + 2 * D_MODEL
OFF_RQ = 0
OFF_RK = OFF_RQ + RET_QK_W
OFF_RV = OFF_RK + RET_QK_W
OFF_RG = OFF_RV + RET_V_W
OFF_MQ = OFF_RG + RET_V_W
OFF_MK = OFF_MQ + MOBA_W
OFF_MV = OFF_MK + MOBA_W
OFF_GATE = OFF_MV + MOBA_W

LANES = 128
ROW_BLOCK = 256
UNIT_BLOCKS = 5
UNIT_ROWS = UNIT_BLOCKS * ROW_BLOCK
MIB = 1024 * 1024


def _cparams(sem, vmem_mib):
    return pltpu.CompilerParams(dimension_semantics=sem, vmem_limit_bytes=vmem_mib * MIB)


def _split_dot(a, b, dims):
    ah = a.astype(BF16)
    al = (a - ah.astype(F32)).astype(BF16)
    bh = b.astype(BF16)
    bl = (b - bh.astype(F32)).astype(BF16)
    d = functools.partial(lax.dot_general, dimension_numbers=dims, preferred_element_type=F32)
    return d(ah, bh) + (d(ah, bl) + d(al, bh))


_NN = (((1,), (0,)), ((), ()))
_NT = (((1,), (1,)), ((), ()))


def _rope_kernel(pos_ref, inv_ref, cos_ref, sin_ref):
    ang = pos_ref[...].astype(F32) * inv_ref[...]
    lane = lax.broadcasted_iota(jnp.int32, ang.shape, 1)
    cos_ref[...] = jnp.cos(ang)
    sin_ref[...] = jnp.where(lane < LANES // 2, -jnp.sin(ang), jnp.sin(ang))


def _rope_tables(pos_col, inv_full):
    t = pos_col.shape[0]
    tm = 1024
    return pl.pallas_call(
        _rope_kernel,
        grid=(t // tm,),
        in_specs=[pl.BlockSpec((tm, 1), lambda i: (i, 0)),
                  pl.BlockSpec((1, LANES), lambda i: (0, 0))],
        out_specs=[pl.BlockSpec((tm, LANES), lambda i: (i, 0))] * 2,
        out_shape=[jax.ShapeDtypeStruct((t, LANES), F32)] * 2,
        compiler_params=_cparams(("arbitrary",), 16),
        name="rope_tables",
    )(pos_col, inv_full)


_IN_TM = 1024
_IN_TN = 512


def _in_proj_kernel(x_ref, g_ref, w_ref, cos_ref, sin_ref, o_ref, h_ref):
    j = pl.program_id(1)

    @pl.when(j == 0)
    def _():
        x = x_ref[...]
        ms = jnp.mean(x * x, axis=-1, keepdims=True)
        h_ref[...] = (x * lax.rsqrt(ms + RMS_EPS) * g_ref[...]).astype(BF16)

    acc = jnp.dot(h_ref[...], w_ref[...], preferred_element_type=F32)
    col = j * _IN_TN
    is_rk = (col >= OFF_RK) & (col < OFF_RV)
    is_rot = (col < OFF_RV) | ((col >= OFF_MQ) & (col < OFF_MV))
    is_sig = col >= OFF_GATE

    @pl.when(is_rot)
    def _():
        scale = jnp.where(is_rk, RET_QK_DIM ** -0.5, 1.0).astype(F32)
        cos = cos_ref[...]
        sin = sin_ref[...]
        for hh in range(_IN_TN // LANES):
            blk = acc[:, hh * LANES:(hh + 1) * LANES]
            rot = blk * cos + pltpu.roll(blk, LANES // 2, 1) * sin
            o_ref[:, hh * LANES:(hh + 1) * LANES] = (rot * scale).astype(BF16)

    @pl.when(is_sig)
    def _():
        o_ref[...] = jax.nn.sigmoid(acc).astype(BF16)

    @pl.when(jnp.logical_not(is_rot | is_sig))
    def _():
        o_ref[...] = acc.astype(BF16)


def _in_proj(x2, g, w_bf, cos_t, sin_t):
    t = x2.shape[0]
    return pl.pallas_call(
        _in_proj_kernel,
        grid=(t // _IN_TM, IN_WIDTH // _IN_TN),
        in_specs=[pl.BlockSpec((_IN_TM, D_MODEL), lambda i, j: (i, 0)),
                  pl.BlockSpec((1, D_MODEL), lambda i, j: (0, 0)),
                  pl.BlockSpec((D_MODEL, _IN_TN), lambda i, j: (0, j)),
                  pl.BlockSpec((_IN_TM, LANES), lambda i, j: (i, 0)),
                  pl.BlockSpec((_IN_TM, LANES), lambda i, j: (i, 0))],
        out_specs=pl.BlockSpec((_IN_TM, _IN_TN), lambda i, j: (i, j)),
        out_shape=jax.ShapeDtypeStruct((t, IN_WIDTH), BF16),
        scratch_shapes=[pltpu.VMEM((_IN_TM, D_MODEL), BF16)],
        compiler_params=_cparams(("arbitrary", "arbitrary"), 48),
        name="in_proj",
    )(x2, g, w_bf, cos_t, sin_t)


def _retention_kernel(lg_ref, q_ref, k_ref, v_ref, g_ref, o_ref, state_ref):
    c = RET_CHUNK
    lg = lg_ref[pl.program_id(1)]
    ii = lax.broadcasted_iota(jnp.int32, (c, c), 0)
    jj = lax.broadcasted_iota(jnp.int32, (c, c), 1)
    diff = (ii - jj).astype(F32)
    decay = jnp.where(diff >= 0.0, jnp.exp(jnp.maximum(diff, 0.0) * lg), 0.0)
    pos = lax.broadcasted_iota(jnp.int32, (c, 1), 0).astype(F32)
    k_w = jnp.exp((c - 1.0 - pos) * lg)
    q_w = jnp.exp((pos + 1.0) * lg)
    chunk_decay = jnp.exp(jnp.full((1, RET_V_DIM), float(c), F32) * lg)
    state_ref[...] = jnp.zeros_like(state_ref)

    def body(n, carry):
        sl = pl.ds(pl.multiple_of(n * c, c), c)
        q = q_ref[sl, :]
        k = k_ref[sl, :]
        v = v_ref[sl, :]
        s = lax.dot_general(q, k, _NT, preferred_element_type=F32) * decay
        o = jnp.dot(s.astype(BF16), v, preferred_element_type=F32)
        state = state_ref[...]
        qs = (q.astype(F32) * q_w).astype(BF16)
        o = o + jnp.dot(qs, state.astype(BF16), preferred_element_type=F32)
        kt = (k.astype(F32) * k_w).T.astype(BF16)
        state_ref[...] = chunk_decay * state + jnp.dot(kt, v, preferred_element_type=F32)
        on = o * lax.rsqrt(jnp.mean(o * o, axis=-1, keepdims=True) + RMS_EPS)
        gg = g_ref[sl, :].astype(F32)
        o_ref[sl, :] = (on * (gg * jax.nn.sigmoid(gg))).astype(BF16)
        return carry

    lax.fori_loop(0, q_ref.shape[0] // c, body, 0)


def _retention(log_gamma, proj, batch, seq):
    qb, vb = OFF_RQ // RET_QK_DIM, OFF_RV // RET_V_DIM
    kb, gb = OFF_RK // RET_QK_DIM, OFF_RG // RET_V_DIM
    return pl.pallas_call(
        _retention_kernel,
        grid_spec=pltpu.PrefetchScalarGridSpec(
            num_scalar_prefetch=1,
            grid=(batch, RET_HEADS),
            in_specs=[pl.BlockSpec((seq, RET_QK_DIM), lambda b, h, lg: (b, qb + h)),
                      pl.BlockSpec((seq, RET_QK_DIM), lambda b, h, lg: (b, kb + h)),
                      pl.BlockSpec((seq, RET_V_DIM), lambda b, h, lg: (b, vb + h)),
                      pl.BlockSpec((seq, RET_V_DIM), lambda b, h, lg: (b, gb + h))],
            out_specs=pl.BlockSpec((seq, RET_V_DIM), lambda b, h, lg: (b, h)),
            scratch_shapes=[pltpu.VMEM((RET_QK_DIM, RET_V_DIM), F32)]),
        out_shape=jax.ShapeDtypeStruct((batch * seq, RET_V_W), BF16),
        compiler_params=_cparams(("arbitrary", "arbitrary"), 32),
        name="retention",
    )(log_gamma, proj, proj, proj, proj)


def _moba_kernel(q_ref, k_ref, v_ref, o_ref):
    L = MOBA_BLOCK
    seq = q_ref.shape[0]
    n_kb = seq // L
    scale = MOBA_HEAD_DIM ** -0.5
    kmean = jnp.concatenate(
        [jnp.mean(k_ref[j * L:(j + 1) * L, :].astype(F32), axis=0, keepdims=True) for j in range(n_kb)]
        + [jnp.zeros((LANES - n_kb, MOBA_HEAD_DIM), F32)], axis=0)
    qi_idx = lax.broadcasted_iota(jnp.int32, (L, L), 0)
    kj_idx = lax.broadcasted_iota(jnp.int32, (L, L), 1)
    causal = kj_idx <= qi_idx
    for i in range(n_kb):
        q = q_ref[i * L:(i + 1) * L, :]
        sel = None
        if i > MOBA_TOPK:
            gate = _split_dot(q.astype(F32), kmean, _NT)
            cols = [gate[:, j:j + 1] for j in range(i)]
            sel = []
            for j in range(i):
                rank = jnp.zeros((L, 1), F32)
                for j2 in range(i):
                    if j2 == j:
                        continue
                    ahead = (cols[j2] > cols[j]) | ((cols[j2] == cols[j]) & (j2 < j))
                    rank = rank + ahead.astype(F32)
                sel.append(rank < float(MOBA_TOPK))
        s_blocks = []
        m = None
        for j in range(i + 1):
            s = lax.dot_general(q, k_ref[j * L:(j + 1) * L, :], _NT, preferred_element_type=F32) * scale
            if j == i:
                s = jnp.where(causal, s, NEG_INF)
            elif sel is not None:
                s = jnp.where(sel[j], s, NEG_INF)
            s_blocks.append(s)
            mj = jnp.max(s, axis=-1, keepdims=True)
            m = mj if m is None else jnp.maximum(m, mj)
        acc = jnp.zeros((L, MOBA_HEAD_DIM), F32)
        denom = jnp.zeros((L, 1), F32)
        for j in range(i + 1):
            p = jnp.exp(s_blocks[j] - m)
            denom = denom + jnp.sum(p, axis=-1, keepdims=True)
            acc = acc + jnp.dot(p.astype(BF16), v_ref[j * L:(j + 1) * L, :], preferred_element_type=F32)
        o_ref[i * L:(i + 1) * L, :] = (acc / denom).astype(BF16)


def _moba(proj, batch, seq):
    qb, kb, vb = OFF_MQ // MOBA_HEAD_DIM, OFF_MK // MOBA_HEAD_DIM, OFF_MV // MOBA_HEAD_DIM
    return pl.pallas_call(
        _moba_kernel,
        grid=(batch, MOBA_HEADS),
        in_specs=[pl.BlockSpec((seq, MOBA_HEAD_DIM), lambda b, h: (b, qb + h)),
                  pl.BlockSpec((seq, MOBA_HEAD_DIM), lambda b, h: (b, kb + h)),
                  pl.BlockSpec((seq, MOBA_HEAD_DIM), lambda b, h: (b, vb + h))],
        out_specs=pl.BlockSpec((seq, MOBA_HEAD_DIM), lambda b, h: (b, h)),
        out_shape=jax.ShapeDtypeStruct((batch * seq, MOBA_W), BF16),
        compiler_params=_cparams(("arbitrary", "arbitrary"), 32),
        name="moba",
    )(proj, proj, proj)


_OUT_TM = 256


def _out_proj_kernel(oret_ref, omoba_ref, g1_ref, g2_ref, x_ref, wr_ref, wm_ref, wo_ref,
                     gn_ref, rw_ref, rb_ref, x1_ref, h2_ref, lg_ref):
    yr = jnp.dot(oret_ref[...], wr_ref[...], preferred_element_type=F32)
    ym = jnp.dot(omoba_ref[...], wm_ref[...], preferred_element_type=F32)
    merged = g1_ref[...].astype(F32) * yr + g2_ref[...].astype(F32) * ym
    x1 = x_ref[...] + jnp.dot(merged.astype(BF16), wo_ref[...], preferred_element_type=F32)
    x1_ref[...] = x1
    h2 = x1 * lax.rsqrt(jnp.mean(x1 * x1, axis=-1, keepdims=True) + RMS_EPS) * gn_ref[...]
    h2_ref[...] = h2
    lg_ref[...] = _split_dot(h2, rw_ref[...], _NN) + rb_ref[...]


def _out_proj(o_ret, o_moba, proj, x2, wr, wm, wo, gn, rw_pad, rb_pad):
    t = x2.shape[0]
    gblk = OFF_GATE // D_MODEL
    row = lambda i: (i, 0)
    const = lambda i: (0, 0)
    wspec = pl.BlockSpec((D_MODEL, D_MODEL), const, pipeline_mode=pl.Buffered(1))
    return pl.pallas_call(
        _out_proj_kernel,
        grid=(t // _OUT_TM,),
        in_specs=[pl.BlockSpec((_OUT_TM, D_MODEL), row),
                  pl.BlockSpec((_OUT_TM, D_MODEL), row),
                  pl.BlockSpec((_OUT_TM, D_MODEL), lambda i: (i, gblk)),
                  pl.BlockSpec((_OUT_TM, D_MODEL), lambda i: (i, gblk + 1)),
                  pl.BlockSpec((_OUT_TM, D_MODEL), row),
                  wspec, wspec, wspec,
                  pl.BlockSpec((1, D_MODEL), const),
                  pl.BlockSpec((D_MODEL, LANES), const),
                  pl.BlockSpec((1, LANES), const)],
        out_specs=[pl.BlockSpec((_OUT_TM, D_MODEL), row),
                   pl.BlockSpec((_OUT_TM, D_MODEL), row),
                   pl.BlockSpec((_OUT_TM, LANES), row)],
        out_shape=[jax.ShapeDtypeStruct((t, D_MODEL), F32),
                   jax.ShapeDtypeStruct((t, D_MODEL), F32),
                   jax.ShapeDtypeStruct((t, LANES), F32)],
        compiler_params=_cparams(("arbitrary",), 56),
        name="out_proj",
    )(o_ret, o_moba, proj, proj, x2, wr, wm, wo, gn, rw_pad, rb_pad)


_RT_CHUNK = 256
_MAX_UNITS = 64


def _lane_scan(x, lane):
    s = 1
    while s < LANES:
        x = x + jnp.where(lane >= s, pltpu.roll(x, s, 1), 0.0)
        s *= 2
    return x


def _routing_kernel(lg_ref, dest_ref, comb_ref, unit_ref, idx_s, pos_s):
    t = lg_ref.shape[0]
    ch = _RT_CHUNK
    n_ch = t // ch
    lane_i = lax.broadcasted_iota(jnp.int32, (ch, LANES), 1)
    lane = lane_i.astype(F32)
    ri = lax.broadcasted_iota(jnp.int32, (ch, ch), 0)
    ci = lax.broadcasted_iota(jnp.int32, (ch, ch), 1)
    tri = (ci < ri).astype(BF16)

    def phase1(c, carry):
        sl = pl.ds(pl.multiple_of(c * ch, ch), ch)
        l = lg_ref[sl, :]
        onehot = jnp.zeros((ch, LANES), F32)
        vals, idxs = [], []
        for _ in range(TOP_K):
            m = jnp.max(l, axis=-1, keepdims=True)
            idx = jnp.min(jnp.where(l == m, lane, float(LANES)), axis=-1, keepdims=True)
            hit = lane == idx
            vals.append(m)
            idxs.append(idx)
            onehot = onehot + hit.astype(F32)
            l = jnp.where(hit, -jnp.inf, l)
        exps = [jnp.exp(v - vals[0]) for v in vals]
        denom = exps[0] + exps[1] + exps[2] + exps[3]
        before = jnp.dot(tri, onehot.astype(BF16), preferred_element_type=F32) + carry
        idx_row = jnp.zeros((ch, LANES), F32)
        pos_row = jnp.zeros((ch, LANES), F32)
        comb_row = jnp.zeros((ch, LANES), F32)
        for k in range(TOP_K):
            pos_k = jnp.sum(jnp.where(lane == idxs[k], before, 0.0), axis=-1, keepdims=True)
            idx_row = jnp.where(lane_i == k, idxs[k], idx_row)
            pos_row = jnp.where(lane_i == k, pos_k, pos_row)
            comb_row = jnp.where(lane_i == k, exps[k] / denom, comb_row)
        idx_s[sl, :] = idx_row
        pos_s[sl, :] = pos_row
        comb_ref[sl, :] = comb_row
        return carry + jnp.sum(onehot, axis=0, keepdims=True)

    counts = lax.fori_loop(0, n_ch, phase1, jnp.zeros((1, LANES), F32))
    lane8 = lax.broadcasted_iota(jnp.int32, (8, LANES), 1)
    counts8 = jnp.broadcast_to(counts, (8, LANES))
    nblk = jnp.floor((counts8 + (ROW_BLOCK - 1.0)) * (1.0 / ROW_BLOCK))
    padded = nblk * float(ROW_BLOCK)
    pstart = _lane_scan(padded, lane8) - padded
    pstart_row = pstart[0:1, :]

    def phase2(c, carry):
        sl = pl.ds(pl.multiple_of(c * ch, ch), ch)
        idx_row = idx_s[sl, :]
        pos_row = pos_s[sl, :]
        dest_row = jnp.zeros((ch, LANES), F32)
        for k in range(TOP_K):
            start_k = jnp.sum(jnp.where(lane == idx_row[:, k:k + 1], pstart_row, 0.0), axis=-1, keepdims=True)
            dest_row = jnp.where(lane_i == k, start_k + pos_row[:, k:k + 1], dest_row)
        dest_ref[sl, :] = dest_row.astype(jnp.int32)
        return carry

    lax.fori_loop(0, n_ch, phase2, 0)

    units_e = jnp.floor((nblk + (UNIT_BLOCKS - 0.5)) * (1.0 / UNIT_BLOCKS))
    ucum = _lane_scan(units_e, lane8)
    ucum_row = ucum[0:1, :]
    uexcl_row = ucum_row - units_e[0:1, :]
    nblk_row = nblk[0:1, :]
    pblk_row = pstart_row * (1.0 / ROW_BLOCK)
    lane_u = lax.broadcasted_iota(jnp.int32, (_MAX_UNITS, LANES), 1)
    lane_uf = lane_u.astype(F32)
    uu = lax.broadcasted_iota(jnp.int32, (_MAX_UNITS, LANES), 0).astype(F32)
    e_u = jnp.sum(jnp.where((ucum_row <= uu) & (lane_u < N_EXPERTS), 1.0, 0.0), axis=-1, keepdims=True)
    e_u = jnp.minimum(e_u, N_EXPERTS - 1.0)
    pick = lane_uf == e_u
    take = lambda row: jnp.sum(jnp.where(pick, row, 0.0), axis=-1, keepdims=True)
    k_in_e = uu[:, 0:1] - take(uexcl_row)
    nb_u = jnp.clip(take(nblk_row) - k_in_e * UNIT_BLOCKS, 0.0, float(UNIT_BLOCKS))
    rb0_u = take(pblk_row) + k_in_e * UNIT_BLOCKS
    table = jnp.where(lane_u == 0, e_u, jnp.where(lane_u == 1, rb0_u, jnp.where(lane_u == 2, nb_u, 0.0)))
    unit_ref[...] = table.astype(jnp.int32)


def _routing(logits):
    t = logits.shape[0]
    return pl.pallas_call(
        _routing_kernel,
        out_shape=[jax.ShapeDtypeStruct((t, LANES), jnp.int32),
                   jax.ShapeDtypeStruct((t, LANES), F32),
                   jax.ShapeDtypeStruct((_MAX_UNITS, LANES), jnp.int32)],
        scratch_shapes=[pltpu.VMEM((t, LANES), F32), pltpu.VMEM((t, LANES), F32)],
        compiler_params=pltpu.CompilerParams(vmem_limit_bytes=48 * MIB),
        name="routing",
    )(logits)


_DISP_TOK = 256


def _row_copy(src_hbm, s, dst_hbm, d, sem):
    return pltpu.make_async_copy(src_hbm.at[pl.ds(s, 1)], dst_hbm.at[pl.ds(d, 1)], sem)


def _dispatch_kernel(dest_ref, h2_hbm, zero_hbm, xr_hbm, sem):
    del zero_hbm
    base = pl.program_id(0) * _DISP_TOK

    def issue(tk, carry):
        _row_copy(h2_hbm, base + tk // TOP_K, xr_hbm, dest_ref[tk], sem).start()
        return carry

    lax.fori_loop(0, _DISP_TOK * TOP_K, issue, 0)

    def drain(tk, carry):
        _row_copy(h2_hbm, 0, xr_hbm, 0, sem).wait()
        return carry

    lax.fori_loop(0, _DISP_TOK * TOP_K, drain, 0)


def _dispatch(dest_flat, h2, n_rows):
    t = h2.shape[0]
    zeros = jnp.zeros((n_rows, D_MODEL), F32)
    return pl.pallas_call(
        _dispatch_kernel,
        grid=(t // _DISP_TOK,),
        in_specs=[pl.BlockSpec((_DISP_TOK * TOP_K,), lambda i: (i,), memory_space=pltpu.SMEM),
                  pl.BlockSpec(memory_space=pl.ANY),
                  pl.BlockSpec(memory_space=pl.ANY)],
        out_specs=pl.BlockSpec(memory_space=pl.ANY),
        out_shape=jax.ShapeDtypeStruct((n_rows, D_MODEL), F32),
        scratch_shapes=[pltpu.SemaphoreType.DMA(())],
        input_output_aliases={2: 0},
        compiler_params=_cparams(("arbitrary",), 16),
        name="dispatch",
    )(dest_flat, h2, zeros)


_EXP_TF = 256
_EXP_J = D_FF // _EXP_TF


def _blk_copy(src, s, dst, d, sem):
    return pltpu.make_async_copy(src.at[pl.ds(s, ROW_BLOCK)], dst.at[pl.ds(d, ROW_BLOCK)], sem)


def _experts_kernel(ue_ref, urb_ref, unb_ref, xr_hbm, wg_ref, wu_ref, bg_ref, bu_ref, wd_ref, bd_ref,
                    y_hbm, xf_ref, xb_ref, acc_ref, wg_bf, wu_bf, wd_bf, sem_in, sem_out):
    u = pl.program_id(0)
    j = pl.program_id(1)
    nb = unb_ref[u]
    row0 = urb_ref[u] * ROW_BLOCK

    @pl.when((u == 0) & (j == 0))
    def _():
        xb_ref[...] = jnp.zeros_like(xb_ref)

    @pl.when(nb > 0)
    def _():
        @pl.when(j == 0)
        def _():
            def start(r, c):
                _blk_copy(xr_hbm, row0 + r * ROW_BLOCK, xf_ref, r * ROW_BLOCK, sem_in).start()
                return c

            lax.fori_loop(0, nb, start, 0)

            def finish(r, c):
                _blk_copy(xr_hbm, 0, xf_ref, 0, sem_in).wait()
                return c

            lax.fori_loop(0, nb, finish, 0)

            def cast(r, c):
                sl = pl.ds(pl.multiple_of(r * ROW_BLOCK, ROW_BLOCK), ROW_BLOCK)
                xb_ref[sl, :] = xf_ref[sl, :].astype(BF16)
                return c

            lax.fori_loop(0, nb, cast, 0)

        wg_bf[...] = wg_ref[...].astype(BF16)
        wu_bf[...] = wu_ref[...].astype(BF16)
        wd_bf[...] = wd_ref[...].astype(BF16)
        x = xb_ref[...]
        g = jnp.dot(x, wg_bf[...], preferred_element_type=F32) + bg_ref[...]
        up = jnp.dot(x, wu_bf[...], preferred_element_type=F32) + bu_ref[...]
        gate = jnp.minimum(g, SWIGLU_LIMIT)
        up = jnp.clip(up, -SWIGLU_LIMIT, SWIGLU_LIMIT)
        act = (up + 1.0) * gate * jax.nn.sigmoid(SWIGLU_ALPHA * gate)
        part = jnp.dot(act.astype(BF16), wd_bf[...], preferred_element_type=F32)

        @pl.when(j == 0)
        def _():
            acc_ref[...] = part + bd_ref[...]

        @pl.when(j > 0)
        def _():
            acc_ref[...] += part

        @pl.when(j == _EXP_J - 1)
        def _():
            def start(r, c):
                _blk_copy(acc_ref, r * ROW_BLOCK, y_hbm, row0 + r * ROW_BLOCK, sem_out).start()
                return c

            lax.fori_loop(0, nb, start, 0)

            def finish(r, c):
                _blk_copy(acc_ref, 0, y_hbm, 0, sem_out).wait()
                return c

            lax.fori_loop(0, nb, finish, 0)


def _experts(unit_e, unit_rb, unit_nb, x_rows, w_gu, b_gu3, w_down, b_down3):
    n_rows = x_rows.shape[0]
    jlast = _EXP_J - 1

    def jj(u, j, nb):
        return jnp.where(nb[u] > 0, j, jlast)

    tf = _EXP_TF
    return pl.pallas_call(
        _experts_kernel,
        grid_spec=pltpu.PrefetchScalarGridSpec(
            num_scalar_prefetch=3,
            grid=(_MAX_UNITS, _EXP_J),
            in_specs=[pl.BlockSpec(memory_space=pl.ANY),
                      pl.BlockSpec((None, D_MODEL, tf), lambda u, j, e, rb, nb: (e[u], 0, jj(u, j, nb))),
                      pl.BlockSpec((None, D_MODEL, tf), lambda u, j, e, rb, nb: (e[u], 0, _EXP_J + jj(u, j, nb))),
                      pl.BlockSpec((None, 1, tf), lambda u, j, e, rb, nb: (e[u], 0, jj(u, j, nb))),
                      pl.BlockSpec((None, 1, tf), lambda u, j, e, rb, nb: (e[u], 0, _EXP_J + jj(u, j, nb))),
                      pl.BlockSpec((None, tf, D_MODEL), lambda u, j, e, rb, nb: (e[u], jj(u, j, nb), 0)),
                      pl.BlockSpec((None, 1, D_MODEL), lambda u, j, e, rb, nb: (e[u], 0, 0))],
            out_specs=pl.BlockSpec(memory_space=pl.ANY),
            scratch_shapes=[pltpu.VMEM((UNIT_ROWS, D_MODEL), F32),
                            pltpu.VMEM((UNIT_ROWS, D_MODEL), BF16),
                            pltpu.VMEM((UNIT_ROWS, D_MODEL), F32),
                            pltpu.VMEM((D_MODEL, tf), BF16),
                            pltpu.VMEM((D_MODEL, tf), BF16),
                            pltpu.VMEM((tf, D_MODEL), BF16),
                            pltpu.SemaphoreType.DMA(()),
                            pltpu.SemaphoreType.DMA(())]),
        out_shape=jax.ShapeDtypeStruct((n_rows, D_MODEL), F32),
        input_output_aliases={3: 0},
        compiler_params=_cparams(("arbitrary", "arbitrary"), 58),
        name="experts",
    )(unit_e, unit_rb, unit_nb, x_rows, w_gu, w_gu, b_gu3, b_gu3, w_down, b_down3)


_CMB_TOK = 256


def _combine_kernel(dest_ref, x1_ref, comb_ref, gf_ref, y_hbm, o_ref, ybuf, sem):
    def issue(tk, carry):
        tok = tk // TOP_K
        k = tk % TOP_K
        pltpu.make_async_copy(y_hbm.at[pl.ds(dest_ref[tk], 1)], ybuf.at[k, pl.ds(tok, 1)], sem).start()
        return carry

    lax.fori_loop(0, _CMB_TOK * TOP_K, issue, 0)

    def drain(tk, carry):
        pltpu.make_async_copy(y_hbm.at[pl.ds(0, 1)], ybuf.at[0, pl.ds(0, 1)], sem).wait()
        return carry

    lax.fori_loop(0, _CMB_TOK * TOP_K, drain, 0)

    comb = comb_ref[...]
    x2 = x1_ref[...]
    for k in range(TOP_K):
        x2 = x2 + comb[:, k:k + 1] * ybuf[k]
    o_ref[...] = x2 * lax.rsqrt(jnp.mean(x2 * x2, axis=-1, keepdims=True) + RMS_EPS) * gf_ref[...]


def _combine(dest_flat, x1, comb, g_final, y_rows):
    t = x1.shape[0]
    row = lambda i: (i, 0)
    return pl.pallas_call(
        _combine_kernel,
        grid=(t // _CMB_TOK,),
        in_specs=[pl.BlockSpec((_CMB_TOK * TOP_K,), lambda i: (i,), memory_space=pltpu.SMEM),
                  pl.BlockSpec((_CMB_TOK, D_MODEL), row),
                  pl.BlockSpec((_CMB_TOK, LANES), row),
                  pl.BlockSpec((1, D_MODEL), lambda i: (0, 0)),
                  pl.BlockSpec(memory_space=pl.ANY)],
        out_specs=pl.BlockSpec((_CMB_TOK, D_MODEL), row),
        out_shape=jax.ShapeDtypeStruct((t, D_MODEL), F32),
        scratch_shapes=[pltpu.VMEM((TOP_K, _CMB_TOK, D_MODEL), F32), pltpu.SemaphoreType.DMA(())],
        compiler_params=_cparams(("arbitrary",), 32),
        name="combine",
    )(dest_flat, x1, comb, g_final, y_rows)


def kernel(x, positions, norm_mix_g, w_in, ret_w_o, moba_w_o, w_out, norm_ffn_g, router_w, router_b,
           exp_w_gu, exp_b_gu, exp_w_down, exp_b_down, norm_final_g):
    batch, seq, d = x.shape
    t = batch * seq
    depth = w_in.shape[0]
    assert depth == 1, "the combine kernel applies the closing norm, so exactly one layer is supported"
    half = MOBA_HEAD_DIM // 2
    inv_freq = ROPE_THETA ** (-jnp.arange(half, dtype=F32) / half)
    inv_full = jnp.concatenate([inv_freq, inv_freq])[None, :]
    log_gamma = jnp.log1p(-jnp.exp2(-5.0 - jnp.arange(RET_HEADS, dtype=F32)))
    cos_t, sin_t = _rope_tables(positions.reshape(t, 1), inv_full)
    n_rows = (-(-(t * TOP_K) // ROW_BLOCK) + N_EXPERTS) * ROW_BLOCK

    x2 = x.reshape(t, d)
    for l in range(depth):
        proj = _in_proj(x2, norm_mix_g[l][None, :], w_in[l].astype(BF16), cos_t, sin_t)
        o_ret = _retention(log_gamma, proj, batch, seq)
        o_moba = _moba(proj, batch, seq)
        rw_pad = jnp.pad(router_w[l], ((0, 0), (0, LANES - N_EXPERTS)))
        rb_pad = jnp.pad(router_b[l], (0, LANES - N_EXPERTS), constant_values=NEG_INF)[None, :]
        x1, h2, logits = _out_proj(o_ret, o_moba, proj, x2, ret_w_o[l].astype(BF16), moba_w_o[l].astype(BF16),
                                   w_out[l].astype(BF16), norm_ffn_g[l][None, :], rw_pad, rb_pad)
        dest, comb, units = _routing(logits)
        dest_flat = dest[:, :TOP_K].reshape(t * TOP_K)
        x_rows = _dispatch(dest_flat, h2, n_rows)
        y_rows = _experts(units[:, 0], units[:, 1], units[:, 2], x_rows, exp_w_gu[l],
                          exp_b_gu[l][:, None, :], exp_w_down[l], exp_b_down[l][:, None, :])
        x2 = _combine(dest_flat, x1, comb, norm_final_g[None, :], y_rows)
    return x2.reshape(batch, seq, d)
```

```python
import functools

import jax
import jax.numpy as jnp
from jax import lax
from jax.experimental import pallas as pl
from jax.experimental.pallas import tpu as pltpu

F32 = jnp.float32
BF16 = jnp.bfloat16
U32 = jnp.uint32

D_MODEL = 2048
RET_HEADS = 8
RET_QK_DIM = 128
RET_V_DIM = 256
RET_CHUNK = 128
MOBA_HEADS = 16
MOBA_HEAD_DIM = 128
MOBA_BLOCK = 256
MOBA_TOPK = 3
ROPE_THETA = 10000.0
NEG_INF = -1e30
N_EXPERTS = 32
TOP_K = 4
D_FF = D_MODEL
SWIGLU_LIMIT = 7.0
SWIGLU_ALPHA = 1.702
RMS_EPS = 1e-5

RET_QK_W = RET_HEADS * RET_QK_DIM
RET_V_W = RET_HEADS * RET_V_DIM
MOBA_W = MOBA_HEADS * MOBA_HEAD_DIM
IN_WIDTH = 2 * RET_QK_W + 2 * RET_V_W + 3 * MOBA_W + 2 * D_MODEL
OFF_RQ = 0
OFF_RK = OFF_RQ + RET_QK_W
OFF_RV = OFF_RK + RET_QK_W
OFF_RG = OFF_RV + RET_V_W
OFF_MQ = OFF_RG + RET_V_W
OFF_MK = OFF_MQ + MOBA_W
OFF_MV = OFF_MK + MOBA_W
OFF_GATE = OFF_MV + MOBA_W

LANES = 128
ROW_BLOCK = 256
UNIT_BLOCKS = 5
UNIT_ROWS = UNIT_BLOCKS * ROW_BLOCK
MIB = 1024 * 1024


def _cparams(sem, vmem_mib):
    return pltpu.CompilerParams(dimension_semantics=sem, vmem_limit_bytes=vmem_mib * MIB)


def _split_dot(a, b, dims):
    ah = a.astype(BF16)
    al = (a - ah.astype(F32)).astype(BF16)
    bh = b.astype(BF16)
    bl = (b - bh.astype(F32)).astype(BF16)
    d = functools.partial(lax.dot_general, dimension_numbers=dims, preferred_element_type=F32)
    return d(ah, bh) + (d(ah, bl) + d(al, bh))


def _pack_halves(x):
    n = x.shape[1] // 2
    bits = lax.bitcast_convert_type(x.astype(BF16).astype(F32), U32)
    return bits[:, :n] | lax.shift_right_logical(bits[:, n:], jnp.uint32(16))


def _unpack_halves(u):
    hi = lax.bitcast_convert_type(u & jnp.uint32(0xFFFF0000), F32).astype(BF16)
    lo = lax.bitcast_convert_type(lax.shift_left(u, jnp.uint32(16)), F32).astype(BF16)
    return hi, lo


_NN = (((1,), (0,)), ((), ()))
_NT = (((1,), (1,)), ((), ()))


def _rope_kernel(pos_ref, inv_ref, cos_ref, sin_ref):
    ang = pos_ref[...].astype(F32) * inv_ref[...]
    lane = lax.broadcasted_iota(jnp.int32, ang.shape, 1)
    cos_ref[...] = jnp.cos(ang)
    sin_ref[...] = jnp.where(lane < LANES // 2, -jnp.sin(ang), jnp.sin(ang))


def _rope_tables(pos_col, inv_full):
    t = pos_col.shape[0]
    tm = 1024
    return pl.pallas_call(
        _rope_kernel,
        grid=(t // tm,),
        in_specs=[pl.BlockSpec((tm, 1), lambda i: (i, 0)),
                  pl.BlockSpec((1, LANES), lambda i: (0, 0))],
        out_specs=[pl.BlockSpec((tm, LANES), lambda i: (i, 0))] * 2,
        out_shape=[jax.ShapeDtypeStruct((t, LANES), F32)] * 2,
        compiler_params=_cparams(("arbitrary",), 16),
        name="rope_tables",
    )(pos_col, inv_full)


_IN_TM = 1024
_IN_TN = 512


def _in_proj_kernel(x_ref, g_ref, w_ref, cos_ref, sin_ref, o_ref, h_ref):
    j = pl.program_id(1)

    @pl.when(j == 0)
    def _():
        x = x_ref[...]
        ms = jnp.mean(x * x, axis=-1, keepdims=True)
        h_ref[...] = (x * lax.rsqrt(ms + RMS_EPS) * g_ref[...]).astype(BF16)

    acc = jnp.dot(h_ref[...], w_ref[...], preferred_element_type=F32)
    col = j * _IN_TN
    is_rk = (col >= OFF_RK) & (col < OFF_RV)
    is_rot = (col < OFF_RV) | ((col >= OFF_MQ) & (col < OFF_MV))
    is_sig = col >= OFF_GATE

    @pl.when(is_rot)
    def _():
        scale = jnp.where(is_rk, RET_QK_DIM ** -0.5, 1.0).astype(F32)
        cos = cos_ref[...]
        sin = sin_ref[...]
        for hh in range(_IN_TN // LANES):
            blk = acc[:, hh * LANES:(hh + 1) * LANES]
            rot = blk * cos + pltpu.roll(blk, LANES // 2, 1) * sin
            o_ref[:, hh * LANES:(hh + 1) * LANES] = (rot * scale).astype(BF16)

    @pl.when(is_sig)
    def _():
        o_ref[...] = jax.nn.sigmoid(acc).astype(BF16)

    @pl.when(jnp.logical_not(is_rot | is_sig))
    def _():
        o_ref[...] = acc.astype(BF16)


def _in_proj(x2, g, w_bf, cos_t, sin_t):
    t = x2.shape[0]
    return pl.pallas_call(
        _in_proj_kernel,
        grid=(t // _IN_TM, IN_WIDTH // _IN_TN),
        in_specs=[pl.BlockSpec((_IN_TM, D_MODEL), lambda i, j: (i, 0)),
                  pl.BlockSpec((1, D_MODEL), lambda i, j: (0, 0)),
                  pl.BlockSpec((D_MODEL, _IN_TN), lambda i, j: (0, j)),
                  pl.BlockSpec((_IN_TM, LANES), lambda i, j: (i, 0)),
                  pl.BlockSpec((_IN_TM, LANES), lambda i, j: (i, 0))],
        out_specs=pl.BlockSpec((_IN_TM, _IN_TN), lambda i, j: (i, j)),
        out_shape=jax.ShapeDtypeStruct((t, IN_WIDTH), BF16),
        scratch_shapes=[pltpu.VMEM((_IN_TM, D_MODEL), BF16)],
        compiler_params=_cparams(("arbitrary", "arbitrary"), 48),
        name="in_proj",
    )(x2, g, w_bf, cos_t, sin_t)


def _retention_kernel(lg_ref, q_ref, k_ref, v_ref, g_ref, o_ref, state_ref):
    c = RET_CHUNK
    lg = lg_ref[pl.program_id(1)]
    ii = lax.broadcasted_iota(jnp.int32, (c, c), 0)
    jj = lax.broadcasted_iota(jnp.int32, (c, c), 1)
    diff = (ii - jj).astype(F32)
    decay = jnp.where(diff >= 0.0, jnp.exp(jnp.maximum(diff, 0.0) * lg), 0.0)
    pos = lax.broadcasted_iota(jnp.int32, (c, 1), 0).astype(F32)
    k_w = jnp.exp((c - 1.0 - pos) * lg)
    q_w = jnp.exp((pos + 1.0) * lg)
    chunk_decay = jnp.exp(jnp.full((1, RET_V_DIM), float(c), F32) * lg)
    state_ref[...] = jnp.zeros_like(state_ref)

    def body(n, carry):
        sl = pl.ds(pl.multiple_of(n * c, c), c)
        q = q_ref[sl, :]
        k = k_ref[sl, :]
        v = v_ref[sl, :]
        s = lax.dot_general(q, k, _NT, preferred_element_type=F32) * decay
        o = jnp.dot(s.astype(BF16), v, preferred_element_type=F32)
        state = state_ref[...]
        qs = (q.astype(F32) * q_w).astype(BF16)
        o = o + jnp.dot(qs, state.astype(BF16), preferred_element_type=F32)
        kt = (k.astype(F32) * k_w).T.astype(BF16)
        state_ref[...] = chunk_decay * state + jnp.dot(kt, v, preferred_element_type=F32)
        on = o * lax.rsqrt(jnp.mean(o * o, axis=-1, keepdims=True) + RMS_EPS)
        gg = g_ref[sl, :].astype(F32)
        o_ref[sl, :] = (on * (gg * jax.nn.sigmoid(gg))).astype(BF16)
        return carry

    lax.fori_loop(0, q_ref.shape[0] // c, body, 0)


def _retention(log_gamma, proj, batch, seq):
    qb, vb = OFF_RQ // RET_QK_DIM, OFF_RV // RET_V_DIM
    kb, gb = OFF_RK // RET_QK_DIM, OFF_RG // RET_V_DIM
    return pl.pallas_call(
        _retention_kernel,
        grid_spec=pltpu.PrefetchScalarGridSpec(
            num_scalar_prefetch=1,
            grid=(batch, RET_HEADS),
            in_specs=[pl.BlockSpec((seq, RET_QK_DIM), lambda b, h, lg: (b, qb + h)),
                      pl.BlockSpec((seq, RET_QK_DIM), lambda b, h, lg: (b, kb + h)),
                      pl.BlockSpec((seq, RET_V_DIM), lambda b, h, lg: (b, vb + h)),
                      pl.BlockSpec((seq, RET_V_DIM), lambda b, h, lg: (b, gb + h))],
            out_specs=pl.BlockSpec((seq, RET_V_DIM), lambda b, h, lg: (b, h)),
            scratch_shapes=[pltpu.VMEM((RET_QK_DIM, RET_V_DIM), F32)]),
        out_shape=jax.ShapeDtypeStruct((batch * seq, RET_V_W), BF16),
        compiler_params=_cparams(("arbitrary", "arbitrary"), 32),
        name="retention",
    )(log_gamma, proj, proj, proj, proj)


def _moba_kernel(q_ref, k_ref, v_ref, o_ref):
    L = MOBA_BLOCK
    seq = q_ref.shape[0]
    n_kb = seq // L
    scale = MOBA_HEAD_DIM ** -0.5
    kmean = jnp.concatenate(
        [jnp.mean(k_ref[j * L:(j + 1) * L, :].astype(F32), axis=0, keepdims=True) for j in range(n_kb)]
        + [jnp.zeros((LANES - n_kb, MOBA_HEAD_DIM), F32)], axis=0)
    qi_idx = lax.broadcasted_iota(jnp.int32, (L, L), 0)
    kj_idx = lax.broadcasted_iota(jnp.int32, (L, L), 1)
    causal = kj_idx <= qi_idx
    for i in range(n_kb):
        q = q_ref[i * L:(i + 1) * L, :]
        sel = None
        if i > MOBA_TOPK:
            gate = _split_dot(q.astype(F32), kmean, _NT)
            cols = [gate[:, j:j + 1] for j in range(i)]
            sel = []
            for j in range(i):
                rank = jnp.zeros((L, 1), F32)
                for j2 in range(i):
                    if j2 == j:
                        continue
                    ahead = (cols[j2] > cols[j]) | ((cols[j2] == cols[j]) & (j2 < j))
                    rank = rank + ahead.astype(F32)
                sel.append(rank < float(MOBA_TOPK))
        s_blocks = []
        m = None
        for j in range(i + 1):
            s = lax.dot_general(q, k_ref[j * L:(j + 1) * L, :], _NT, preferred_element_type=F32) * scale
            if j == i:
                s = jnp.where(causal, s, NEG_INF)
            elif sel is not None:
                s = jnp.where(sel[j], s, NEG_INF)
            s_blocks.append(s)
            mj = jnp.max(s, axis=-1, keepdims=True)
            m = mj if m is None else jnp.maximum(m, mj)
        acc = jnp.zeros((L, MOBA_HEAD_DIM), F32)
        denom = jnp.zeros((L, 1), F32)
        for j in range(i + 1):
            p = jnp.exp(s_blocks[j] - m)
            denom = denom + jnp.sum(p, axis=-1, keepdims=True)
            acc = acc + jnp.dot(p.astype(BF16), v_ref[j * L:(j + 1) * L, :], preferred_element_type=F32)
        o_ref[i * L:(i + 1) * L, :] = (acc / denom).astype(BF16)


def _moba(proj, batch, seq):
    qb, kb, vb = OFF_MQ // MOBA_HEAD_DIM, OFF_MK // MOBA_HEAD_DIM, OFF_MV // MOBA_HEAD_DIM
    return pl.pallas_call(
        _moba_kernel,
        grid=(batch, MOBA_HEADS),
        in_specs=[pl.BlockSpec((seq, MOBA_HEAD_DIM), lambda b, h: (b, qb + h)),
                  pl.BlockSpec((seq, MOBA_HEAD_DIM), lambda b, h: (b, kb + h)),
                  pl.BlockSpec((seq, MOBA_HEAD_DIM), lambda b, h: (b, vb + h))],
        out_specs=pl.BlockSpec((seq, MOBA_HEAD_DIM), lambda b, h: (b, h)),
        out_shape=jax.ShapeDtypeStruct((batch * seq, MOBA_W), BF16),
        compiler_params=_cparams(("arbitrary", "arbitrary"), 32),
        name="moba",
    )(proj, proj, proj)


_OUT_TM = 256


def _out_proj_kernel(oret_ref, omoba_ref, g1_ref, g2_ref, x_ref, wr_ref, wm_ref, wo_ref,
                     gn_ref, rw_ref, rb_ref, x1_ref, h2_ref, lg_ref):
    yr = jnp.dot(oret_ref[...], wr_ref[...], preferred_element_type=F32)
    ym = jnp.dot(omoba_ref[...], wm_ref[...], preferred_element_type=F32)
    merged = g1_ref[...].astype(F32) * yr + g2_ref[...].astype(F32) * ym
    x1 = x_ref[...] + jnp.dot(merged.astype(BF16), wo_ref[...], preferred_element_type=F32)
    x1_ref[...] = x1
    h2 = x1 * lax.rsqrt(jnp.mean(x1 * x1, axis=-1, keepdims=True) + RMS_EPS) * gn_ref[...]
    h2_ref[...] = _pack_halves(h2)
    lg_ref[...] = _split_dot(h2, rw_ref[...], _NN) + rb_ref[...]


def _out_proj(o_ret, o_moba, proj, x2, wr, wm, wo, gn, rw_pad, rb_pad):
    t = x2.shape[0]
    gblk = OFF_GATE // D_MODEL
    row = lambda i: (i, 0)
    const = lambda i: (0, 0)
    wspec = pl.BlockSpec((D_MODEL, D_MODEL), const, pipeline_mode=pl.Buffered(1))
    return pl.pallas_call(
        _out_proj_kernel,
        grid=(t // _OUT_TM,),
        in_specs=[pl.BlockSpec((_OUT_TM, D_MODEL), row),
                  pl.BlockSpec((_OUT_TM, D_MODEL), row),
                  pl.BlockSpec((_OUT_TM, D_MODEL), lambda i: (i, gblk)),
                  pl.BlockSpec((_OUT_TM, D_MODEL), lambda i: (i, gblk + 1)),
                  pl.BlockSpec((_OUT_TM, D_MODEL), row),
                  wspec, wspec, wspec,
                  pl.BlockSpec((1, D_MODEL), const),
                  pl.BlockSpec((D_MODEL, LANES), const),
                  pl.BlockSpec((1, LANES), const)],
        out_specs=[pl.BlockSpec((_OUT_TM, D_MODEL), row),
                   pl.BlockSpec((_OUT_TM, D_MODEL // 2), row),
                   pl.BlockSpec((_OUT_TM, LANES), row)],
        out_shape=[jax.ShapeDtypeStruct((t, D_MODEL), F32),
                   jax.ShapeDtypeStruct((t, D_MODEL // 2), U32),
                   jax.ShapeDtypeStruct((t, LANES), F32)],
        compiler_params=_cparams(("arbitrary",), 56),
        name="out_proj",
    )(o_ret, o_moba, proj, proj, x2, wr, wm, wo, gn, rw_pad, rb_pad)


_RT_CHUNK = 256
_MAX_UNITS = 64


def _lane_scan(x, lane):
    s = 1
    while s < LANES:
        x = x + jnp.where(lane >= s, pltpu.roll(x, s, 1), 0.0)
        s *= 2
    return x


def _routing_kernel(lg_ref, dest_ref, comb_ref, unit_ref, idx_s, pos_s):
    t = lg_ref.shape[0]
    ch = _RT_CHUNK
    n_ch = t // ch
    lane_i = lax.broadcasted_iota(jnp.int32, (ch, LANES), 1)
    lane = lane_i.astype(F32)
    ri = lax.broadcasted_iota(jnp.int32, (ch, ch), 0)
    ci = lax.broadcasted_iota(jnp.int32, (ch, ch), 1)
    tri = (ci < ri).astype(BF16)

    def phase1(c, carry):
        sl = pl.ds(pl.multiple_of(c * ch, ch), ch)
        l = lg_ref[sl, :]
        onehot = jnp.zeros((ch, LANES), F32)
        vals, idxs = [], []
        for _ in range(TOP_K):
            m = jnp.max(l, axis=-1, keepdims=True)
            idx = jnp.min(jnp.where(l == m, lane, float(LANES)), axis=-1, keepdims=True)
            hit = lane == idx
            vals.append(m)
            idxs.append(idx)
            onehot = onehot + hit.astype(F32)
            l = jnp.where(hit, -jnp.inf, l)
        exps = [jnp.exp(v - vals[0]) for v in vals]
        denom = exps[0] + exps[1] + exps[2] + exps[3]
        before = jnp.dot(tri, onehot.astype(BF16), preferred_element_type=F32) + carry
        idx_row = jnp.zeros((ch, LANES), F32)
        pos_row = jnp.zeros((ch, LANES), F32)
        comb_row = jnp.zeros((ch, LANES), F32)
        for k in range(TOP_K):
            pos_k = jnp.sum(jnp.where(lane == idxs[k], before, 0.0), axis=-1, keepdims=True)
            idx_row = jnp.where(lane_i == k, idxs[k], idx_row)
            pos_row = jnp.where(lane_i == k, pos_k, pos_row)
            comb_row = jnp.where(lane_i == k, exps[k] / denom, comb_row)
        idx_s[sl, :] = idx_row
        pos_s[sl, :] = pos_row
        comb_ref[sl, :] = comb_row
        return carry + jnp.sum(onehot, axis=0, keepdims=True)

    counts = lax.fori_loop(0, n_ch, phase1, jnp.zeros((1, LANES), F32))
    lane8 = lax.broadcasted_iota(jnp.int32, (8, LANES), 1)
    counts8 = jnp.broadcast_to(counts, (8, LANES))
    nblk = jnp.floor((counts8 + (ROW_BLOCK - 1.0)) * (1.0 / ROW_BLOCK))
    padded = nblk * float(ROW_BLOCK)
    pstart = _lane_scan(padded, lane8) - padded
    pstart_row = pstart[0:1, :]

    def phase2(c, carry):
        sl = pl.ds(pl.multiple_of(c * ch, ch), ch)
        idx_row = idx_s[sl, :]
        pos_row = pos_s[sl, :]
        dest_row = jnp.zeros((ch, LANES), F32)
        for k in range(TOP_K):
            start_k = jnp.sum(jnp.where(lane == idx_row[:, k:k + 1], pstart_row, 0.0), axis=-1, keepdims=True)
            dest_row = jnp.where(lane_i == k, start_k + pos_row[:, k:k + 1], dest_row)
        dest_ref[sl, :] = dest_row.astype(jnp.int32)
        return carry

    lax.fori_loop(0, n_ch, phase2, 0)

    units_e = jnp.floor((nblk + (UNIT_BLOCKS - 0.5)) * (1.0 / UNIT_BLOCKS))
    ucum = _lane_scan(units_e, lane8)
    ucum_row = ucum[0:1, :]
    uexcl_row = ucum_row - units_e[0:1, :]
    nblk_row = nblk[0:1, :]
    pblk_row = pstart_row * (1.0 / ROW_BLOCK)
    lane_u = lax.broadcasted_iota(jnp.int32, (_MAX_UNITS, LANES), 1)
    lane_uf = lane_u.astype(F32)
    uu = lax.broadcasted_iota(jnp.int32, (_MAX_UNITS, LANES), 0).astype(F32)
    e_u = jnp.sum(jnp.where((ucum_row <= uu) & (lane_u < N_EXPERTS), 1.0, 0.0), axis=-1, keepdims=True)
    e_u = jnp.minimum(e_u, N_EXPERTS - 1.0)
    pick = lane_uf == e_u
    take = lambda row: jnp.sum(jnp.where(pick, row, 0.0), axis=-1, keepdims=True)
    k_in_e = uu[:, 0:1] - take(uexcl_row)
    nb_u = jnp.clip(take(nblk_row) - k_in_e * UNIT_BLOCKS, 0.0, float(UNIT_BLOCKS))
    rb0_u = take(pblk_row) + k_in_e * UNIT_BLOCKS
    total_blk = jnp.sum(jnp.where(lane_u < N_EXPERTS, nblk_row, 0.0), axis=-1, keepdims=True)
    table = jnp.where(lane_u == 0, e_u, jnp.where(lane_u == 1, rb0_u, jnp.where(lane_u == 2, nb_u,
                      jnp.where(lane_u == 3, total_blk, 0.0))))
    unit_ref[...] = table.astype(jnp.int32)


def _routing(logits):
    t = logits.shape[0]
    return pl.pallas_call(
        _routing_kernel,
        out_shape=[jax.ShapeDtypeStruct((t, LANES), jnp.int32),
                   jax.ShapeDtypeStruct((t, LANES), F32),
                   jax.ShapeDtypeStruct((_MAX_UNITS, LANES), jnp.int32)],
        scratch_shapes=[pltpu.VMEM((t, LANES), F32), pltpu.VMEM((t, LANES), F32)],
        compiler_params=pltpu.CompilerParams(vmem_limit_bytes=48 * MIB),
        name="routing",
    )(logits)


_DISP_TOK = 256


def _dispatch_kernel(dest_ref, h2_ref, zero_hbm, xr_hbm, sem):
    del zero_hbm

    def row_copy(tk):
        return pltpu.make_async_copy(h2_ref.at[pl.ds(tk // TOP_K, 1)], xr_hbm.at[pl.ds(dest_ref[tk], 1)], sem)

    def issue(tk, carry):
        row_copy(tk).start()
        return carry

    lax.fori_loop(0, _DISP_TOK * TOP_K, issue, 0, unroll=8)

    def drain(tk, carry):
        row_copy(tk).wait()
        return carry

    lax.fori_loop(0, _DISP_TOK * TOP_K, drain, 0, unroll=8)


def _dispatch(dest_flat, h2p, n_rows):
    t, w = h2p.shape
    zeros = jnp.zeros((n_rows, w), U32)
    return pl.pallas_call(
        _dispatch_kernel,
        grid=(t // _DISP_TOK,),
        in_specs=[pl.BlockSpec((_DISP_TOK * TOP_K,), lambda i: (i,), memory_space=pltpu.SMEM),
                  pl.BlockSpec((_DISP_TOK, w), lambda i: (i, 0)),
                  pl.BlockSpec(memory_space=pl.ANY)],
        out_specs=pl.BlockSpec(memory_space=pl.ANY),
        out_shape=jax.ShapeDtypeStruct((n_rows, w), U32),
        scratch_shapes=[pltpu.SemaphoreType.DMA(())],
        input_output_aliases={2: 0},
        compiler_params=_cparams(("arbitrary",), 16),
        name="dispatch",
    )(dest_flat, h2p, zeros)


_EXP_TF = 256
_EXP_J = D_FF // _EXP_TF


def _blk_copy(src, s, dst, d, sem):
    return pltpu.make_async_copy(src.at[pl.ds(s, ROW_BLOCK)], dst.at[pl.ds(d, ROW_BLOCK)], sem)


def _experts_kernel(ue_ref, urb_ref, unb_ref, utot_ref, xr_hbm, wg_ref, wu_ref, bg_ref, bu_ref, wd_ref, bd_ref,
                    y_hbm, xf_ref, xb_ref, acc_ref, wg_bf, wu_bf, wd_bf, sem_in, sem_out):
    u = pl.program_id(0)
    j = pl.program_id(1)
    nb = unb_ref[u]
    row0 = urb_ref[u] * ROW_BLOCK
    half = D_MODEL // 2

    @pl.when((u == 0) & (j == 0))
    def _():
        xb_ref[...] = jnp.zeros_like(xb_ref)

    @pl.when(nb > 0)
    def _():
        @pl.when(j == 0)
        def _():
            def start(r, c):
                _blk_copy(xr_hbm, row0 + r * ROW_BLOCK, xf_ref, r * ROW_BLOCK, sem_in).start()
                return c

            lax.fori_loop(0, nb, start, 0)

            def finish(r, c):
                _blk_copy(xr_hbm, 0, xf_ref, 0, sem_in).wait()
                return c

            lax.fori_loop(0, nb, finish, 0)

            def cast(r, c):
                sl = pl.ds(pl.multiple_of(r * ROW_BLOCK, ROW_BLOCK), ROW_BLOCK)
                hi, lo = _unpack_halves(xf_ref[sl, :])
                xb_ref[sl, :half] = hi
                xb_ref[sl, half:] = lo
                return c

            lax.fori_loop(0, nb, cast, 0)

        wg_bf[...] = wg_ref[...].astype(BF16)
        wu_bf[...] = wu_ref[...].astype(BF16)
        wd_bf[...] = wd_ref[...].astype(BF16)
        x = xb_ref[...]
        g = jnp.dot(x, wg_bf[...], preferred_element_type=F32) + bg_ref[...]
        up = jnp.dot(x, wu_bf[...], preferred_element_type=F32) + bu_ref[...]
        gate = jnp.minimum(g, SWIGLU_LIMIT)
        up = jnp.clip(up, -SWIGLU_LIMIT, SWIGLU_LIMIT)
        act = (up + 1.0) * gate * jax.nn.sigmoid(SWIGLU_ALPHA * gate)
        part = jnp.dot(act.astype(BF16), wd_bf[...], preferred_element_type=F32)

        @pl.when(j == 0)
        def _():
            acc_ref[...] = part + bd_ref[...]

        @pl.when(j > 0)
        def _():
            acc_ref[...] += part

        @pl.when(j == _EXP_J - 1)
        def _():
            def start(r, c):
                _blk_copy(acc_ref, r * ROW_BLOCK, y_hbm, row0 + r * ROW_BLOCK, sem_out).start()
                return c

            lax.fori_loop(0, nb, start, 0)

            def finish(r, c):
                _blk_copy(acc_ref, 0, y_hbm, 0, sem_out).wait()
                return c

            lax.fori_loop(0, nb, finish, 0)

    @pl.when((u == _MAX_UNITS - 1) & (j == _EXP_J - 1))
    def _():
        acc_ref[0:ROW_BLOCK, :] = jnp.zeros((ROW_BLOCK, D_MODEL), F32)
        first = utot_ref[0]
        n_blocks = y_hbm.shape[0] // ROW_BLOCK

        def start(r, c):
            _blk_copy(acc_ref, 0, y_hbm, r * ROW_BLOCK, sem_out).start()
            return c

        lax.fori_loop(first, n_blocks, start, 0)

        def finish(r, c):
            _blk_copy(acc_ref, 0, y_hbm, 0, sem_out).wait()
            return c

        lax.fori_loop(first, n_blocks, finish, 0)


def _experts(unit_e, unit_rb, unit_nb, unit_tot, x_rows, w_gu, b_gu3, w_down, b_down3):
    n_rows = x_rows.shape[0]
    jlast = _EXP_J - 1

    def jj(u, j, nb):
        return jnp.where(nb[u] > 0, j, jlast)

    tf = _EXP_TF
    return pl.pallas_call(
        _experts_kernel,
        grid_spec=pltpu.PrefetchScalarGridSpec(
            num_scalar_prefetch=4,
            grid=(_MAX_UNITS, _EXP_J),
            in_specs=[pl.BlockSpec(memory_space=pl.ANY),
                      pl.BlockSpec((None, D_MODEL, tf), lambda u, j, e, rb, nb, tot: (e[u], 0, jj(u, j, nb))),
                      pl.BlockSpec((None, D_MODEL, tf),
                                   lambda u, j, e, rb, nb, tot: (e[u], 0, _EXP_J + jj(u, j, nb))),
                      pl.BlockSpec((None, 1, tf), lambda u, j, e, rb, nb, tot: (e[u], 0, jj(u, j, nb))),
                      pl.BlockSpec((None, 1, tf), lambda u, j, e, rb, nb, tot: (e[u], 0, _EXP_J + jj(u, j, nb))),
                      pl.BlockSpec((None, tf, D_MODEL), lambda u, j, e, rb, nb, tot: (e[u], jj(u, j, nb), 0)),
                      pl.BlockSpec((None, 1, D_MODEL), lambda u, j, e, rb, nb, tot: (e[u], 0, 0))],
            out_specs=pl.BlockSpec(memory_space=pl.ANY),
            scratch_shapes=[pltpu.VMEM((UNIT_ROWS, D_MODEL // 2), U32),
                            pltpu.VMEM((UNIT_ROWS, D_MODEL), BF16),
                            pltpu.VMEM((UNIT_ROWS, D_MODEL), F32),
                            pltpu.VMEM((D_MODEL, tf), BF16),
                            pltpu.VMEM((D_MODEL, tf), BF16),
                            pltpu.VMEM((tf, D_MODEL), BF16),
                            pltpu.SemaphoreType.DMA(()),
                            pltpu.SemaphoreType.DMA(())]),
        out_shape=jax.ShapeDtypeStruct((n_rows, D_MODEL), F32),
        compiler_params=_cparams(("arbitrary", "arbitrary"), 58),
        name="experts",
    )(unit_e, unit_rb, unit_nb, unit_tot, x_rows, w_gu, w_gu, b_gu3, b_gu3, w_down, b_down3)


_CMB_TOK = 256


def _combine_kernel(dest_ref, x1_ref, comb_ref, gf_ref, y_hbm, o_ref, ybuf, sem):
    def issue(tk, carry):
        tok = tk // TOP_K
        k = tk % TOP_K
        pltpu.make_async_copy(y_hbm.at[pl.ds(dest_ref[tk], 1)], ybuf.at[k, pl.ds(tok, 1)], sem).start()
        return carry

    lax.fori_loop(0, _CMB_TOK * TOP_K, issue, 0)

    def drain(tk, carry):
        pltpu.make_async_copy(y_hbm.at[pl.ds(0, 1)], ybuf.at[0, pl.ds(0, 1)], sem).wait()
        return carry

    lax.fori_loop(0, _CMB_TOK * TOP_K, drain, 0)

    comb = comb_ref[...]
    x2 = x1_ref[...]
    for k in range(TOP_K):
        x2 = x2 + comb[:, k:k + 1] * ybuf[k]
    o_ref[...] = x2 * lax.rsqrt(jnp.mean(x2 * x2, axis=-1, keepdims=True) + RMS_EPS) * gf_ref[...]


def _combine(dest_flat, x1, comb, g_final, y_rows):
    t = x1.shape[0]
    row = lambda i: (i, 0)
    return pl.pallas_call(
        _combine_kernel,
        grid=(t // _CMB_TOK,),
        in_specs=[pl.BlockSpec((_CMB_TOK * TOP_K,), lambda i: (i,), memory_space=pltpu.SMEM),
                  pl.BlockSpec((_CMB_TOK, D_MODEL), row),
                  pl.BlockSpec((_CMB_TOK, LANES), row),
                  pl.BlockSpec((1, D_MODEL), lambda i: (0, 0)),
                  pl.BlockSpec(memory_space=pl.ANY)],
        out_specs=pl.BlockSpec((_CMB_TOK, D_MODEL), row),
        out_shape=jax.ShapeDtypeStruct((t, D_MODEL), F32),
        scratch_shapes=[pltpu.VMEM((TOP_K, _CMB_TOK, D_MODEL), F32), pltpu.SemaphoreType.DMA(())],
        compiler_params=_cparams(("arbitrary",), 32),
        name="combine",
    )(dest_flat, x1, comb, g_final, y_rows)


def kernel(x, positions, norm_mix_g, w_in, ret_w_o, moba_w_o, w_out, norm_ffn_g, router_w, router_b,
           exp_w_gu, exp_b_gu, exp_w_down, exp_b_down, norm_final_g):
    batch, seq, d = x.shape
    t = batch * seq
    depth = w_in.shape[0]
    assert depth == 1, "the combine kernel applies the closing norm, so exactly one layer is supported"
    half = MOBA_HEAD_DIM // 2
    inv_freq = ROPE_THETA ** (-jnp.arange(half, dtype=F32) / half)
    inv_full = jnp.concatenate([inv_freq, inv_freq])[None, :]
    log_gamma = jnp.log1p(-jnp.exp2(-5.0 - jnp.arange(RET_HEADS, dtype=F32)))
    cos_t, sin_t = _rope_tables(positions.reshape(t, 1), inv_full)
    n_rows = (-(-(t * TOP_K) // ROW_BLOCK) + N_EXPERTS) * ROW_BLOCK

    x2 = x.reshape(t, d)
    for l in range(depth):
        proj = _in_proj(x2, norm_mix_g[l][None, :], w_in[l].astype(BF16), cos_t, sin_t)
        o_ret = _retention(log_gamma, proj, batch, seq)
        o_moba = _moba(proj, batch, seq)
        rw_pad = jnp.pad(router_w[l], ((0, 0), (0, LANES - N_EXPERTS)))
        rb_pad = jnp.pad(router_b[l], (0, LANES - N_EXPERTS), constant_values=NEG_INF)[None, :]
        x1, h2, logits = _out_proj(o_ret, o_moba, proj, x2, ret_w_o[l].astype(BF16), moba_w_o[l].astype(BF16),
                                   w_out[l].astype(BF16), norm_ffn_g[l][None, :], rw_pad, rb_pad)
        dest, comb, units = _routing(logits)
        dest_flat = dest[:, :TOP_K].reshape(t * TOP_K)
        x_rows = _dispatch(dest_flat, h2, n_rows)
        y_rows = _experts(units[:, 0], units[:, 1], units[:, 2], units[0:1, 3], x_rows, exp_w_gu[l],
                          exp_b_gu[l][:, None, :], exp_w_down[l], exp_b_down[l][:, None, :])
        x2 = _combine(dest_flat, x1, comb, norm_final_g[None, :], y_rows)
    return x2.reshape(batch, seq, d)
```

```python
import functools

import jax
import jax.numpy as jnp
from jax import lax
from jax.experimental import pallas as pl
from jax.experimental.pallas import tpu as pltpu

F32 = jnp.float32
BF16 = jnp.bfloat16
U32 = jnp.uint32

D_MODEL = 2048
RET_HEADS = 8
RET_QK_DIM = 128
RET_V_DIM = 256
RET_CHUNK = 128
MOBA_HEADS = 16
MOBA_HEAD_DIM = 128
MOBA_BLOCK = 256
MOBA_TOPK = 3
ROPE_THETA = 10000.0
NEG_INF = -1e30
N_EXPERTS = 32
TOP_K = 4
D_FF = D_MODEL
SWIGLU_LIMIT = 7.0
SWIGLU_ALPHA = 1.702
RMS_EPS = 1e-5

RET_QK_W = RET_HEADS * RET_QK_DIM
RET_V_W = RET_HEADS * RET_V_DIM
MOBA_W = MOBA_HEADS * MOBA_HEAD_DIM
IN_WIDTH = 2 * RET_QK_W + 2 * RET_V_W + 3 * MOBA_W + 2 * D_MODEL
OFF_RQ = 0
OFF_RK = OFF_RQ + RET_QK_W
OFF_RV = OFF_RK + RET_QK_W
OFF_RG = OFF_RV + RET_V_W
OFF_MQ = OFF_RG + RET_V_W
OFF_MK = OFF_MQ + MOBA_W
OFF_MV = OFF_MK + MOBA_W
OFF_GATE = OFF_MV + MOBA_W

LANES = 128
ROW_BLOCK = 256
UNIT_BLOCKS = 5
UNIT_ROWS = UNIT_BLOCKS * ROW_BLOCK
MIB = 1024 * 1024


def _cparams(sem, vmem_mib):
    return pltpu.CompilerParams(dimension_semantics=sem, vmem_limit_bytes=vmem_mib * MIB)


def _split_dot(a, b, dims):
    ah = a.astype(BF16)
    al = (a - ah.astype(F32)).astype(BF16)
    bh = b.astype(BF16)
    bl = (b - bh.astype(F32)).astype(BF16)
    d = functools.partial(lax.dot_general, dimension_numbers=dims, preferred_element_type=F32)
    return d(ah, bh) + (d(ah, bl) + d(al, bh))


def _pack_halves(x):
    n = x.shape[1] // 2
    bits = lax.bitcast_convert_type(x.astype(BF16).astype(F32), U32)
    return bits[:, :n] | lax.shift_right_logical(bits[:, n:], jnp.uint32(16))


def _unpack_halves(u):
    hi = lax.bitcast_convert_type(u & jnp.uint32(0xFFFF0000), F32).astype(BF16)
    lo = lax.bitcast_convert_type(lax.shift_left(u, jnp.uint32(16)), F32).astype(BF16)
    return hi, lo


_NN = (((1,), (0,)), ((), ()))
_NT = (((1,), (1,)), ((), ()))


def _rope_kernel(pos_ref, inv_ref, cos_ref, sin_ref):
    ang = pos_ref[...].astype(F32) * inv_ref[...]
    lane = lax.broadcasted_iota(jnp.int32, ang.shape, 1)
    cos_ref[...] = jnp.cos(ang)
    sin_ref[...] = jnp.where(lane < LANES // 2, -jnp.sin(ang), jnp.sin(ang))


def _rope_tables(pos_col, inv_full):
    t = pos_col.shape[0]
    tm = 1024
    return pl.pallas_call(
        _rope_kernel,
        grid=(t // tm,),
        in_specs=[pl.BlockSpec((tm, 1), lambda i: (i, 0)),
                  pl.BlockSpec((1, LANES), lambda i: (0, 0))],
        out_specs=[pl.BlockSpec((tm, LANES), lambda i: (i, 0))] * 2,
        out_shape=[jax.ShapeDtypeStruct((t, LANES), F32)] * 2,
        compiler_params=_cparams(("arbitrary",), 16),
        name="rope_tables",
    )(pos_col, inv_full)


_IN_TM = 1024
_IN_TN = 512


def _in_proj_kernel(x_ref, g_ref, w_ref, cos_ref, sin_ref, o_ref, h_ref):
    j = pl.program_id(1)

    @pl.when(j == 0)
    def _():
        x = x_ref[...]
        ms = jnp.mean(x * x, axis=-1, keepdims=True)
        h_ref[...] = (x * lax.rsqrt(ms + RMS_EPS) * g_ref[...]).astype(BF16)

    acc = jnp.dot(h_ref[...], w_ref[...], preferred_element_type=F32)
    col = j * _IN_TN
    is_rk = (col >= OFF_RK) & (col < OFF_RV)
    is_rot = (col < OFF_RV) | ((col >= OFF_MQ) & (col < OFF_MV))
    is_sig = col >= OFF_GATE

    @pl.when(is_rot)
    def _():
        scale = jnp.where(is_rk, RET_QK_DIM ** -0.5, 1.0).astype(F32)
        cos = cos_ref[...]
        sin = sin_ref[...]
        for hh in range(_IN_TN // LANES):
            blk = acc[:, hh * LANES:(hh + 1) * LANES]
            rot = blk * cos + pltpu.roll(blk, LANES // 2, 1) * sin
            o_ref[:, hh * LANES:(hh + 1) * LANES] = (rot * scale).astype(BF16)

    @pl.when(is_sig)
    def _():
        o_ref[...] = jax.nn.sigmoid(acc).astype(BF16)

    @pl.when(jnp.logical_not(is_rot | is_sig))
    def _():
        o_ref[...] = acc.astype(BF16)


def _in_proj(x2, g, w_bf, cos_t, sin_t):
    t = x2.shape[0]
    return pl.pallas_call(
        _in_proj_kernel,
        grid=(t // _IN_TM, IN_WIDTH // _IN_TN),
        in_specs=[pl.BlockSpec((_IN_TM, D_MODEL), lambda i, j: (i, 0)),
                  pl.BlockSpec((1, D_MODEL), lambda i, j: (0, 0)),
                  pl.BlockSpec((D_MODEL, _IN_TN), lambda i, j: (0, j)),
                  pl.BlockSpec((_IN_TM, LANES), lambda i, j: (i, 0)),
                  pl.BlockSpec((_IN_TM, LANES), lambda i, j: (i, 0))],
        out_specs=pl.BlockSpec((_IN_TM, _IN_TN), lambda i, j: (i, j)),
        out_shape=jax.ShapeDtypeStruct((t, IN_WIDTH), BF16),
        scratch_shapes=[pltpu.VMEM((_IN_TM, D_MODEL), BF16)],
        compiler_params=_cparams(("arbitrary", "arbitrary"), 48),
        name="in_proj",
    )(x2, g, w_bf, cos_t, sin_t)


def _retention_kernel(lg_ref, q_ref, k_ref, v_ref, g_ref, o_ref, state_ref):
    c = RET_CHUNK
    lg = lg_ref[pl.program_id(1)]
    ii = lax.broadcasted_iota(jnp.int32, (c, c), 0)
    jj = lax.broadcasted_iota(jnp.int32, (c, c), 1)
    diff = (ii - jj).astype(F32)
    decay = jnp.where(diff >= 0.0, jnp.exp(jnp.maximum(diff, 0.0) * lg), 0.0)
    pos = lax.broadcasted_iota(jnp.int32, (c, 1), 0).astype(F32)
    k_w = jnp.exp((c - 1.0 - pos) * lg)
    q_w = jnp.exp((pos + 1.0) * lg)
    chunk_decay = jnp.exp(jnp.full((1, RET_V_DIM), float(c), F32) * lg)
    state_ref[...] = jnp.zeros_like(state_ref)

    def body(n, carry):
        sl = pl.ds(pl.multiple_of(n * c, c), c)
        q = q_ref[sl, :]
        k = k_ref[sl, :]
        v = v_ref[sl, :]
        s = lax.dot_general(q, k, _NT, preferred_element_type=F32) * decay
        o = jnp.dot(s.astype(BF16), v, preferred_element_type=F32)
        state = state_ref[...]
        qs = (q.astype(F32) * q_w).astype(BF16)
        o = o + jnp.dot(qs, state.astype(BF16), preferred_element_type=F32)
        kt = (k.astype(F32) * k_w).T.astype(BF16)
        state_ref[...] = chunk_decay * state + jnp.dot(kt, v, preferred_element_type=F32)
        on = o * lax.rsqrt(jnp.mean(o * o, axis=-1, keepdims=True) + RMS_EPS)
        gg = g_ref[sl, :].astype(F32)
        o_ref[sl, :] = (on * (gg * jax.nn.sigmoid(gg))).astype(BF16)
        return carry

    lax.fori_loop(0, q_ref.shape[0] // c, body, 0)


def _retention(log_gamma, proj, batch, seq):
    qb, vb = OFF_RQ // RET_QK_DIM, OFF_RV // RET_V_DIM
    kb, gb = OFF_RK // RET_QK_DIM, OFF_RG // RET_V_DIM
    return pl.pallas_call(
        _retention_kernel,
        grid_spec=pltpu.PrefetchScalarGridSpec(
            num_scalar_prefetch=1,
            grid=(batch, RET_HEADS),
            in_specs=[pl.BlockSpec((seq, RET_QK_DIM), lambda b, h, lg: (b, qb + h)),
                      pl.BlockSpec((seq, RET_QK_DIM), lambda b, h, lg: (b, kb + h)),
                      pl.BlockSpec((seq, RET_V_DIM), lambda b, h, lg: (b, vb + h)),
                      pl.BlockSpec((seq, RET_V_DIM), lambda b, h, lg: (b, gb + h))],
            out_specs=pl.BlockSpec((seq, RET_V_DIM), lambda b, h, lg: (b, h)),
            scratch_shapes=[pltpu.VMEM((RET_QK_DIM, RET_V_DIM), F32)]),
        out_shape=jax.ShapeDtypeStruct((batch * seq, RET_V_W), BF16),
        compiler_params=_cparams(("arbitrary", "arbitrary"), 32),
        name="retention",
    )(log_gamma, proj, proj, proj, proj)


def _moba_kernel(q_ref, k_ref, v_ref, o_ref):
    L = MOBA_BLOCK
    seq = q_ref.shape[0]
    n_kb = seq // L
    scale = MOBA_HEAD_DIM ** -0.5
    kmean = jnp.concatenate(
        [jnp.mean(k_ref[j * L:(j + 1) * L, :].astype(F32), axis=0, keepdims=True) for j in range(n_kb)]
        + [jnp.zeros((LANES - n_kb, MOBA_HEAD_DIM), F32)], axis=0)
    qi_idx = lax.broadcasted_iota(jnp.int32, (L, L), 0)
    kj_idx = lax.broadcasted_iota(jnp.int32, (L, L), 1)
    causal = kj_idx <= qi_idx
    for i in range(n_kb):
        q = q_ref[i * L:(i + 1) * L, :]
        sel = None
        if i > MOBA_TOPK:
            gate = _split_dot(q.astype(F32), kmean, _NT)
            cols = [gate[:, j:j + 1] for j in range(i)]
            sel = []
            for j in range(i):
                rank = jnp.zeros((L, 1), F32)
                for j2 in range(i):
                    if j2 == j:
                        continue
                    ahead = (cols[j2] > cols[j]) | ((cols[j2] == cols[j]) & (j2 < j))
                    rank = rank + ahead.astype(F32)
                sel.append(rank < float(MOBA_TOPK))
        s_blocks = []
        m = None
        for j in range(i + 1):
            s = lax.dot_general(q, k_ref[j * L:(j + 1) * L, :], _NT, preferred_element_type=F32) * scale
            if j == i:
                s = jnp.where(causal, s, NEG_INF)
            elif sel is not None:
                s = jnp.where(sel[j], s, NEG_INF)
            s_blocks.append(s)
            mj = jnp.max(s, axis=-1, keepdims=True)
            m = mj if m is None else jnp.maximum(m, mj)
        acc = jnp.zeros((L, MOBA_HEAD_DIM), F32)
        denom = jnp.zeros((L, 1), F32)
        for j in range(i + 1):
            p = jnp.exp(s_blocks[j] - m)
            denom = denom + jnp.sum(p, axis=-1, keepdims=True)
            acc = acc + jnp.dot(p.astype(BF16), v_ref[j * L:(j + 1) * L, :], preferred_element_type=F32)
        o_ref[i * L:(i + 1) * L, :] = (acc / denom).astype(BF16)


def _moba(proj, batch, seq):
    qb, kb, vb = OFF_MQ // MOBA_HEAD_DIM, OFF_MK // MOBA_HEAD_DIM, OFF_MV // MOBA_HEAD_DIM
    return pl.pallas_call(
        _moba_kernel,
        grid=(batch, MOBA_HEADS),
        in_specs=[pl.BlockSpec((seq, MOBA_HEAD_DIM), lambda b, h: (b, qb + h)),
                  pl.BlockSpec((seq, MOBA_HEAD_DIM), lambda b, h: (b, kb + h)),
                  pl.BlockSpec((seq, MOBA_HEAD_DIM), lambda b, h: (b, vb + h))],
        out_specs=pl.BlockSpec((seq, MOBA_HEAD_DIM), lambda b, h: (b, h)),
        out_shape=jax.ShapeDtypeStruct((batch * seq, MOBA_W), BF16),
        compiler_params=_cparams(("arbitrary", "arbitrary"), 32),
        name="moba",
    )(proj, proj, proj)


_OUT_TM = 256


def _out_proj_kernel(oret_ref, omoba_ref, g1_ref, g2_ref, x_ref, wr_ref, wm_ref, wo_ref,
                     gn_ref, rw_ref, rb_ref, x1_ref, h2_ref, lg_ref):
    yr = jnp.dot(oret_ref[...], wr_ref[...], preferred_element_type=F32)
    ym = jnp.dot(omoba_ref[...], wm_ref[...], preferred_element_type=F32)
    merged = g1_ref[...].astype(F32) * yr + g2_ref[...].astype(F32) * ym
    x1 = x_ref[...] + jnp.dot(merged.astype(BF16), wo_ref[...], preferred_element_type=F32)
    x1_ref[...] = x1
    h2 = x1 * lax.rsqrt(jnp.mean(x1 * x1, axis=-1, keepdims=True) + RMS_EPS) * gn_ref[...]
    h2_ref[...] = _pack_halves(h2)
    lg_ref[...] = _split_dot(h2, rw_ref[...], _NN) + rb_ref[...]


def _out_proj(o_ret, o_moba, proj, x2, wr, wm, wo, gn, rw_pad, rb_pad):
    t = x2.shape[0]
    gblk = OFF_GATE // D_MODEL
    row = lambda i: (i, 0)
    const = lambda i: (0, 0)
    wspec = pl.BlockSpec((D_MODEL, D_MODEL), const, pipeline_mode=pl.Buffered(1))
    return pl.pallas_call(
        _out_proj_kernel,
        grid=(t // _OUT_TM,),
        in_specs=[pl.BlockSpec((_OUT_TM, D_MODEL), row),
                  pl.BlockSpec((_OUT_TM, D_MODEL), row),
                  pl.BlockSpec((_OUT_TM, D_MODEL), lambda i: (i, gblk)),
                  pl.BlockSpec((_OUT_TM, D_MODEL), lambda i: (i, gblk + 1)),
                  pl.BlockSpec((_OUT_TM, D_MODEL), row),
                  wspec, wspec, wspec,
                  pl.BlockSpec((1, D_MODEL), const),
                  pl.BlockSpec((D_MODEL, LANES), const),
                  pl.BlockSpec((1, LANES), const)],
        out_specs=[pl.BlockSpec((_OUT_TM, D_MODEL), row),
                   pl.BlockSpec((_OUT_TM, D_MODEL // 2), row),
                   pl.BlockSpec((_OUT_TM, LANES), row)],
        out_shape=[jax.ShapeDtypeStruct((t, D_MODEL), F32),
                   jax.ShapeDtypeStruct((t, D_MODEL // 2), U32),
                   jax.ShapeDtypeStruct((t, LANES), F32)],
        compiler_params=_cparams(("arbitrary",), 56),
        name="out_proj",
    )(o_ret, o_moba, proj, proj, x2, wr, wm, wo, gn, rw_pad, rb_pad)


_RT_CHUNK = 256
_MAX_UNITS = 64


def _lane_scan(x, lane):
    s = 1
    while s < LANES:
        x = x + jnp.where(lane >= s, pltpu.roll(x, s, 1), 0.0)
        s *= 2
    return x


def _routing_kernel(lg_ref, dest_ref, comb_ref, unit_ref, idx_s, pos_s):
    t = lg_ref.shape[0]
    ch = _RT_CHUNK
    n_ch = t // ch
    lane_i = lax.broadcasted_iota(jnp.int32, (ch, LANES), 1)
    lane = lane_i.astype(F32)
    ri = lax.broadcasted_iota(jnp.int32, (ch, ch), 0)
    ci = lax.broadcasted_iota(jnp.int32, (ch, ch), 1)
    tri = (ci < ri).astype(BF16)

    def phase1(c, carry):
        sl = pl.ds(pl.multiple_of(c * ch, ch), ch)
        l = lg_ref[sl, :]
        onehot = jnp.zeros((ch, LANES), F32)
        vals, idxs = [], []
        for _ in range(TOP_K):
            m = jnp.max(l, axis=-1, keepdims=True)
            idx = jnp.min(jnp.where(l == m, lane, float(LANES)), axis=-1, keepdims=True)
            hit = lane == idx
            vals.append(m)
            idxs.append(idx)
            onehot = onehot + hit.astype(F32)
            l = jnp.where(hit, -jnp.inf, l)
        exps = [jnp.exp(v - vals[0]) for v in vals]
        denom = exps[0] + exps[1] + exps[2] + exps[3]
        before = jnp.dot(tri, onehot.astype(BF16), preferred_element_type=F32) + carry
        idx_row = jnp.zeros((ch, LANES), F32)
        pos_row = jnp.zeros((ch, LANES), F32)
        comb_row = jnp.zeros((ch, LANES), F32)
        for k in range(TOP_K):
            pos_k = jnp.sum(jnp.where(lane == idxs[k], before, 0.0), axis=-1, keepdims=True)
            idx_row = jnp.where(lane_i == k, idxs[k], idx_row)
            pos_row = jnp.where(lane_i == k, pos_k, pos_row)
            comb_row = jnp.where(lane_i == k, exps[k] / denom, comb_row)
        idx_s[sl, :] = idx_row
        pos_s[sl, :] = pos_row
        comb_ref[sl, :] = comb_row
        return carry + jnp.sum(onehot, axis=0, keepdims=True)

    counts = lax.fori_loop(0, n_ch, phase1, jnp.zeros((1, LANES), F32))
    lane8 = lax.broadcasted_iota(jnp.int32, (8, LANES), 1)
    counts8 = jnp.broadcast_to(counts, (8, LANES))
    nblk = jnp.floor((counts8 + (ROW_BLOCK - 1.0)) * (1.0 / ROW_BLOCK))
    padded = nblk * float(ROW_BLOCK)
    pstart = _lane_scan(padded, lane8) - padded
    pstart_row = pstart[0:1, :]

    def phase2(c, carry):
        sl = pl.ds(pl.multiple_of(c * ch, ch), ch)
        idx_row = idx_s[sl, :]
        pos_row = pos_s[sl, :]
        dest_row = jnp.zeros((ch, LANES), F32)
        for k in range(TOP_K):
            start_k = jnp.sum(jnp.where(lane == idx_row[:, k:k + 1], pstart_row, 0.0), axis=-1, keepdims=True)
            dest_row = jnp.where(lane_i == k, start_k + pos_row[:, k:k + 1], dest_row)
        dest_ref[sl, :] = dest_row.astype(jnp.int32)
        return carry

    lax.fori_loop(0, n_ch, phase2, 0)

    units_e = jnp.floor((nblk + (UNIT_BLOCKS - 0.5)) * (1.0 / UNIT_BLOCKS))
    ucum = _lane_scan(units_e, lane8)
    ucum_row = ucum[0:1, :]
    uexcl_row = ucum_row - units_e[0:1, :]
    nblk_row = nblk[0:1, :]
    pblk_row = pstart_row * (1.0 / ROW_BLOCK)
    lane_u = lax.broadcasted_iota(jnp.int32, (_MAX_UNITS, LANES), 1)
    lane_uf = lane_u.astype(F32)
    uu = lax.broadcasted_iota(jnp.int32, (_MAX_UNITS, LANES), 0).astype(F32)
    e_u = jnp.sum(jnp.where((ucum_row <= uu) & (lane_u < N_EXPERTS), 1.0, 0.0), axis=-1, keepdims=True)
    e_u = jnp.minimum(e_u, N_EXPERTS - 1.0)
    pick = lane_uf == e_u
    take = lambda row: jnp.sum(jnp.where(pick, row, 0.0), axis=-1, keepdims=True)
    k_in_e = uu[:, 0:1] - take(uexcl_row)
    nb_u = jnp.clip(take(nblk_row) - k_in_e * UNIT_BLOCKS, 0.0, float(UNIT_BLOCKS))
    rb0_u = take(pblk_row) + k_in_e * UNIT_BLOCKS
    total_blk = jnp.sum(jnp.where(lane_u < N_EXPERTS, nblk_row, 0.0), axis=-1, keepdims=True)
    table = jnp.where(lane_u == 0, e_u, jnp.where(lane_u == 1, rb0_u, jnp.where(lane_u == 2, nb_u,
                      jnp.where(lane_u == 3, total_blk, 0.0))))
    unit_ref[...] = table.astype(jnp.int32)


def _routing(logits):
    t = logits.shape[0]
    return pl.pallas_call(
        _routing_kernel,
        out_shape=[jax.ShapeDtypeStruct((t, LANES), jnp.int32),
                   jax.ShapeDtypeStruct((t, LANES), F32),
                   jax.ShapeDtypeStruct((_MAX_UNITS, LANES), jnp.int32)],
        scratch_shapes=[pltpu.VMEM((t, LANES), F32), pltpu.VMEM((t, LANES), F32)],
        compiler_params=pltpu.CompilerParams(vmem_limit_bytes=48 * MIB),
        name="routing",
    )(logits)


_DISP_TOK = 256


def _dispatch_kernel(dest_ref, h2_ref, zero_hbm, xr_hbm, sem):
    del zero_hbm

    def row_copy(tk):
        return pltpu.make_async_copy(h2_ref.at[pl.ds(tk // TOP_K, 1)], xr_hbm.at[pl.ds(dest_ref[tk], 1)], sem)

    def issue(tk, carry):
        row_copy(tk).start()
        return carry

    lax.fori_loop(0, _DISP_TOK * TOP_K, issue, 0, unroll=8)
    for _ in range(TOP_K):
        pltpu.make_async_copy(h2_ref, xr_hbm.at[pl.ds(0, _DISP_TOK)], sem).wait()


def _dispatch(dest_flat, h2p, n_rows):
    t, w = h2p.shape
    zeros = jnp.zeros((n_rows, w), U32)
    return pl.pallas_call(
        _dispatch_kernel,
        grid=(t // _DISP_TOK,),
        in_specs=[pl.BlockSpec((_DISP_TOK * TOP_K,), lambda i: (i,), memory_space=pltpu.SMEM),
                  pl.BlockSpec((_DISP_TOK, w), lambda i: (i, 0)),
                  pl.BlockSpec(memory_space=pl.ANY)],
        out_specs=pl.BlockSpec(memory_space=pl.ANY),
        out_shape=jax.ShapeDtypeStruct((n_rows, w), U32),
        scratch_shapes=[pltpu.SemaphoreType.DMA(())],
        input_output_aliases={2: 0},
        compiler_params=_cparams(("arbitrary",), 16),
        name="dispatch",
    )(dest_flat, h2p, zeros)


_EXP_TF = 256
_EXP_J = D_FF // _EXP_TF


def _blk_copy(src, s, dst, d, sem):
    return pltpu.make_async_copy(src.at[pl.ds(s, ROW_BLOCK)], dst.at[pl.ds(d, ROW_BLOCK)], sem)


def _experts_kernel(ue_ref, urb_ref, unb_ref, utot_ref, xr_hbm, wg_ref, wu_ref, bg_ref, bu_ref, wd_ref, bd_ref,
                    y_hbm, xf_ref, xb_ref, act_ref, out_ref, wg_bf, wu_bf, wd_bf, sem_in, sem_out):
    u = pl.program_id(0)
    s = pl.program_id(1)
    nb = unb_ref[u]
    half = D_MODEL // 2
    tf = _EXP_TF
    last_step = 2 * _EXP_J - 1

    def load_x(unit, start):
        def body(r, c):
            cp = _blk_copy(xr_hbm, (urb_ref[unit] + r) * ROW_BLOCK, xf_ref, r * ROW_BLOCK, sem_in)
            cp.start() if start else cp.wait()
            return c

        lax.fori_loop(0, unb_ref[unit], body, 0)

    def store_y(unit, start):
        def body(r, c):
            cp = _blk_copy(out_ref, r * ROW_BLOCK, y_hbm, (urb_ref[unit] + r) * ROW_BLOCK, sem_out)
            cp.start() if start else cp.wait()
            return c

        lax.fori_loop(0, unb_ref[unit], body, 0)

    prev = jnp.maximum(u - 1, 0)
    nxt = jnp.minimum(u + 1, _MAX_UNITS - 1)
    prev_pending = (u > 0) & (unb_ref[prev] > 0)

    @pl.when((u == 0) & (s == 0))
    def _():
        xb_ref[...] = jnp.zeros_like(xb_ref)
        load_x(u, True)

    @pl.when(prev_pending & (((nb > 0) & (s == _EXP_J)) | ((nb == 0) & (s == 0))))
    def _():
        store_y(prev, False)

    @pl.when(nb > 0)
    def _():
        @pl.when(s == 0)
        def _():
            load_x(u, False)

            def cast(r, c):
                sl = pl.ds(pl.multiple_of(r * ROW_BLOCK, ROW_BLOCK), ROW_BLOCK)
                hi, lo = _unpack_halves(xf_ref[sl, :])
                xb_ref[sl, :half] = hi
                xb_ref[sl, half:] = lo
                return c

            lax.fori_loop(0, nb, cast, 0)

        @pl.when(s < _EXP_J)
        def _():
            wg_bf[...] = wg_ref[...].astype(BF16)
            wu_bf[...] = wu_ref[...].astype(BF16)
            x = xb_ref[...]
            g = jnp.dot(x, wg_bf[...], preferred_element_type=F32) + bg_ref[...]
            up = jnp.dot(x, wu_bf[...], preferred_element_type=F32) + bu_ref[...]
            gate = jnp.minimum(g, SWIGLU_LIMIT)
            up = jnp.clip(up, -SWIGLU_LIMIT, SWIGLU_LIMIT)
            act = (up + 1.0) * gate * jax.nn.sigmoid(SWIGLU_ALPHA * gate)
            act_ref[:, pl.ds(pl.multiple_of(s * tf, tf), tf)] = act.astype(BF16)

        @pl.when((s == _EXP_J) & (u + 1 < _MAX_UNITS))
        def _():
            load_x(nxt, True)

        @pl.when(s >= _EXP_J)
        def _():
            wd_bf[...] = wd_ref[...].astype(BF16)
            y = jnp.dot(act_ref[...], wd_bf[...], preferred_element_type=F32) + bd_ref[...]
            out_ref[:, pl.ds(pl.multiple_of((s - _EXP_J) * tf, tf), tf)] = y

        @pl.when(s == last_step)
        def _():
            store_y(u, True)

    @pl.when((u == _MAX_UNITS - 1) & (s == last_step))
    def _():
        @pl.when(nb > 0)
        def _():
            store_y(u, False)

        out_ref[0:ROW_BLOCK, :] = jnp.zeros((ROW_BLOCK, D_MODEL), F32)
        first = utot_ref[0]
        n_blocks = y_hbm.shape[0] // ROW_BLOCK

        def start(r, c):
            _blk_copy(out_ref, 0, y_hbm, r * ROW_BLOCK, sem_out).start()
            return c

        lax.fori_loop(first, n_blocks, start, 0)

        def finish(r, c):
            _blk_copy(out_ref, 0, y_hbm, 0, sem_out).wait()
            return c

        lax.fori_loop(first, n_blocks, finish, 0)


def _experts(unit_e, unit_rb, unit_nb, unit_tot, x_rows, w_gu, b_gu3, w_down, b_down3):
    n_rows = x_rows.shape[0]
    jlast = _EXP_J - 1

    def ja(u, s, nb):
        return jnp.where(nb[u] > 0, jnp.minimum(s, jlast), jlast)

    def jb(u, s, nb):
        return jnp.where(nb[u] > 0, jnp.maximum(s - _EXP_J, 0), jlast)

    tf = _EXP_TF
    return pl.pallas_call(
        _experts_kernel,
        grid_spec=pltpu.PrefetchScalarGridSpec(
            num_scalar_prefetch=4,
            grid=(_MAX_UNITS, 2 * _EXP_J),
            in_specs=[pl.BlockSpec(memory_space=pl.ANY),
                      pl.BlockSpec((None, D_MODEL, tf), lambda u, s, e, rb, nb, tot: (e[u], 0, ja(u, s, nb))),
                      pl.BlockSpec((None, D_MODEL, tf),
                                   lambda u, s, e, rb, nb, tot: (e[u], 0, _EXP_J + ja(u, s, nb))),
                      pl.BlockSpec((None, 1, tf), lambda u, s, e, rb, nb, tot: (e[u], 0, ja(u, s, nb))),
                      pl.BlockSpec((None, 1, tf), lambda u, s, e, rb, nb, tot: (e[u], 0, _EXP_J + ja(u, s, nb))),
                      pl.BlockSpec((None, D_FF, tf), lambda u, s, e, rb, nb, tot: (e[u], 0, jb(u, s, nb))),
                      pl.BlockSpec((None, 1, tf), lambda u, s, e, rb, nb, tot: (e[u], 0, jb(u, s, nb)))],
            out_specs=pl.BlockSpec(memory_space=pl.ANY),
            scratch_shapes=[pltpu.VMEM((UNIT_ROWS, D_MODEL // 2), U32),
                            pltpu.VMEM((UNIT_ROWS, D_MODEL), BF16),
                            pltpu.VMEM((UNIT_ROWS, D_FF), BF16),
                            pltpu.VMEM((UNIT_ROWS, D_MODEL), F32),
                            pltpu.VMEM((D_MODEL, tf), BF16),
                            pltpu.VMEM((D_MODEL, tf), BF16),
                            pltpu.VMEM((D_FF, tf), BF16),
                            pltpu.SemaphoreType.DMA(()),
                            pltpu.SemaphoreType.DMA(())]),
        out_shape=jax.ShapeDtypeStruct((n_rows, D_MODEL), F32),
        compiler_params=_cparams(("arbitrary", "arbitrary"), 58),
        name="experts",
    )(unit_e, unit_rb, unit_nb, unit_tot, x_rows, w_gu, w_gu, b_gu3, b_gu3, w_down, b_down3)


_CMB_TOK = 256


def _combine_kernel(dest_ref, dest_next_ref, x1_ref, comb_ref, gf_ref, y_hbm, o_ref, ybuf, sem):
    i = pl.program_id(0)
    n = pl.num_programs(0)
    slot = i % 2

    def gather(d_ref, into):
        def issue(tk, carry):
            pltpu.make_async_copy(y_hbm.at[pl.ds(d_ref[tk], 1)],
                                  ybuf.at[into, tk % TOP_K, pl.ds(tk // TOP_K, 1)], sem.at[into]).start()
            return carry

        lax.fori_loop(0, _CMB_TOK * TOP_K, issue, 0, unroll=8)

    @pl.when(i == 0)
    def _():
        gather(dest_ref, 0)

    @pl.when(i + 1 < n)
    def _():
        gather(dest_next_ref, 1 - slot)

    for k in range(TOP_K):
        pltpu.make_async_copy(y_hbm.at[pl.ds(0, _CMB_TOK)], ybuf.at[slot, k], sem.at[slot]).wait()

    comb = comb_ref[...]
    x2 = x1_ref[...]
    for k in range(TOP_K):
        x2 = x2 + comb[:, k:k + 1] * ybuf[slot, k]
    o_ref[...] = x2 * lax.rsqrt(jnp.mean(x2 * x2, axis=-1, keepdims=True) + RMS_EPS) * gf_ref[...]


def _combine(dest_flat, x1, comb, g_final, y_rows):
    t = x1.shape[0]
    n = t // _CMB_TOK
    row = lambda i: (i, 0)
    return pl.pallas_call(
        _combine_kernel,
        grid=(n,),
        in_specs=[pl.BlockSpec((_CMB_TOK * TOP_K,), lambda i: (i,), memory_space=pltpu.SMEM),
                  pl.BlockSpec((_CMB_TOK * TOP_K,), lambda i: (jnp.minimum(i + 1, n - 1),),
                               memory_space=pltpu.SMEM),
                  pl.BlockSpec((_CMB_TOK, D_MODEL), row),
                  pl.BlockSpec((_CMB_TOK, LANES), row),
                  pl.BlockSpec((1, D_MODEL), lambda i: (0, 0)),
                  pl.BlockSpec(memory_space=pl.ANY)],
        out_specs=pl.BlockSpec((_CMB_TOK, D_MODEL), row),
        out_shape=jax.ShapeDtypeStruct((t, D_MODEL), F32),
        scratch_shapes=[pltpu.VMEM((2, TOP_K, _CMB_TOK, D_MODEL), F32), pltpu.SemaphoreType.DMA((2,))],
        compiler_params=_cparams(("arbitrary",), 40),
        name="combine",
    )(dest_flat, dest_flat, x1, comb, g_final, y_rows)


def kernel(x, positions, norm_mix_g, w_in, ret_w_o, moba_w_o, w_out, norm_ffn_g, router_w, router_b,
           exp_w_gu, exp_b_gu, exp_w_down, exp_b_down, norm_final_g):
    batch, seq, d = x.shape
    t = batch * seq
    depth = w_in.shape[0]
    assert depth == 1, "the combine kernel applies the closing norm, so exactly one layer is supported"
    half = MOBA_HEAD_DIM // 2
    inv_freq = ROPE_THETA ** (-jnp.arange(half, dtype=F32) / half)
    inv_full = jnp.concatenate([inv_freq, inv_freq])[None, :]
    log_gamma = jnp.log1p(-jnp.exp2(-5.0 - jnp.arange(RET_HEADS, dtype=F32)))
    cos_t, sin_t = _rope_tables(positions.reshape(t, 1), inv_full)
    n_rows = (-(-(t * TOP_K) // ROW_BLOCK) + N_EXPERTS) * ROW_BLOCK

    x2 = x.reshape(t, d)
    for l in range(depth):
        proj = _in_proj(x2, norm_mix_g[l][None, :], w_in[l].astype(BF16), cos_t, sin_t)
        o_ret = _retention(log_gamma, proj, batch, seq)
        o_moba = _moba(proj, batch, seq)
        rw_pad = jnp.pad(router_w[l], ((0, 0), (0, LANES - N_EXPERTS)))
        rb_pad = jnp.pad(router_b[l], (0, LANES - N_EXPERTS), constant_values=NEG_INF)[None, :]
        x1, h2, logits = _out_proj(o_ret, o_moba, proj, x2, ret_w_o[l].astype(BF16), moba_w_o[l].astype(BF16),
                                   w_out[l].astype(BF16), norm_ffn_g[l][None, :], rw_pad, rb_pad)
        dest, comb, units = _routing(logits)
        dest_flat = dest[:, :TOP_K].reshape(t * TOP_K)
        x_rows = _dispatch(dest_flat, h2, n_rows)
        y_rows = _experts(units[:, 0], units[:, 1], units[:, 2], units[0:1, 3], x_rows, exp_w_gu[l],
                          exp_b_gu[l][:, None, :], exp_w_down[l], exp_b_down[l][:, None, :])
        x2 = _combine(dest_flat, x1, comb, norm_final_g[None, :], y_rows)
    return x2.reshape(batch, seq, d)
```

```python
import functools

import jax
import jax.numpy as jnp
from jax import lax
from jax.experimental import pallas as pl
from jax.experimental.pallas import tpu as pltpu

F32 = jnp.float32
BF16 = jnp.bfloat16
U32 = jnp.uint32

D_MODEL = 2048
RET_HEADS = 8
RET_QK_DIM = 128
RET_V_DIM = 256
RET_CHUNK = 128
MOBA_HEADS = 16
MOBA_HEAD_DIM = 128
MOBA_BLOCK = 256
MOBA_TOPK = 3
ROPE_THETA = 10000.0
NEG_INF = -1e30
N_EXPERTS = 32
TOP_K = 4
D_FF = D_MODEL
SWIGLU_LIMIT = 7.0
SWIGLU_ALPHA = 1.702
RMS_EPS = 1e-5

RET_QK_W = RET_HEADS * RET_QK_DIM
RET_V_W = RET_HEADS * RET_V_DIM
MOBA_W = MOBA_HEADS * MOBA_HEAD_DIM
IN_WIDTH = 2 * RET_QK_W + 2 * RET_V_W + 3 * MOBA_W + 2 * D_MODEL
OFF_RQ = 0
OFF_RK = OFF_RQ + RET_QK_W
OFF_RV = OFF_RK + RET_QK_W
OFF_RG = OFF_RV + RET_V_W
OFF_MQ = OFF_RG + RET_V_W
OFF_MK = OFF_MQ + MOBA_W
OFF_MV = OFF_MK + MOBA_W
OFF_GATE = OFF_MV + MOBA_W

LANES = 128
ROW_BLOCK = 256
UNIT_BLOCKS = 5
UNIT_ROWS = UNIT_BLOCKS * ROW_BLOCK
MIB = 1024 * 1024


def _cparams(sem, vmem_mib):
    return pltpu.CompilerParams(dimension_semantics=sem, vmem_limit_bytes=vmem_mib * MIB)


def _split_dot(a, b, dims):
    ah = a.astype(BF16)
    al = (a - ah.astype(F32)).astype(BF16)
    bh = b.astype(BF16)
    bl = (b - bh.astype(F32)).astype(BF16)
    d = functools.partial(lax.dot_general, dimension_numbers=dims, preferred_element_type=F32)
    return d(ah, bh) + (d(ah, bl) + d(al, bh))


def _pack_halves(x):
    n = x.shape[1] // 2
    bits = lax.bitcast_convert_type(x.astype(BF16).astype(F32), U32)
    return bits[:, :n] | lax.shift_right_logical(bits[:, n:], jnp.uint32(16))


def _unpack_halves(u):
    hi = lax.bitcast_convert_type(u & jnp.uint32(0xFFFF0000), F32).astype(BF16)
    lo = lax.bitcast_convert_type(lax.shift_left(u, jnp.uint32(16)), F32).astype(BF16)
    return hi, lo


_NN = (((1,), (0,)), ((), ()))
_NT = (((1,), (1,)), ((), ()))


def _rope_kernel(pos_ref, inv_ref, cos_ref, sin_ref):
    ang = pos_ref[...].astype(F32) * inv_ref[...]
    lane = lax.broadcasted_iota(jnp.int32, ang.shape, 1)
    cos_ref[...] = jnp.cos(ang)
    sin_ref[...] = jnp.where(lane < LANES // 2, -jnp.sin(ang), jnp.sin(ang))


def _rope_tables(pos_col, inv_full):
    t = pos_col.shape[0]
    tm = 1024
    return pl.pallas_call(
        _rope_kernel,
        grid=(t // tm,),
        in_specs=[pl.BlockSpec((tm, 1), lambda i: (i, 0)),
                  pl.BlockSpec((1, LANES), lambda i: (0, 0))],
        out_specs=[pl.BlockSpec((tm, LANES), lambda i: (i, 0))] * 2,
        out_shape=[jax.ShapeDtypeStruct((t, LANES), F32)] * 2,
        compiler_params=_cparams(("arbitrary",), 16),
        name="rope_tables",
    )(pos_col, inv_full)


_IN_TM = 1024
_IN_TN = 1024


def _in_proj_kernel(x_ref, g_ref, w_ref, cos_ref, sin_ref, o_ref, h_ref):
    j = pl.program_id(1)

    @pl.when(j == 0)
    def _():
        x = x_ref[...]
        ms = jnp.mean(x * x, axis=-1, keepdims=True)
        h_ref[...] = (x * lax.rsqrt(ms + RMS_EPS) * g_ref[...]).astype(BF16)

    acc = jnp.dot(h_ref[...], w_ref[...], preferred_element_type=F32)
    col = j * _IN_TN
    is_rk = (col >= OFF_RK) & (col < OFF_RV)
    is_rot = (col < OFF_RV) | ((col >= OFF_MQ) & (col < OFF_MV))
    is_sig = col >= OFF_GATE

    @pl.when(is_rot)
    def _():
        scale = jnp.where(is_rk, RET_QK_DIM ** -0.5, 1.0).astype(F32)
        cos = cos_ref[...]
        sin = sin_ref[...]
        for hh in range(_IN_TN // LANES):
            blk = acc[:, hh * LANES:(hh + 1) * LANES]
            rot = blk * cos + pltpu.roll(blk, LANES // 2, 1) * sin
            o_ref[:, hh * LANES:(hh + 1) * LANES] = (rot * scale).astype(BF16)

    @pl.when(is_sig)
    def _():
        o_ref[...] = jax.nn.sigmoid(acc).astype(BF16)

    @pl.when(jnp.logical_not(is_rot | is_sig))
    def _():
        o_ref[...] = acc.astype(BF16)


def _in_proj(x2, g, w_bf, cos_t, sin_t):
    t = x2.shape[0]
    return pl.pallas_call(
        _in_proj_kernel,
        grid=(t // _IN_TM, IN_WIDTH // _IN_TN),
        in_specs=[pl.BlockSpec((_IN_TM, D_MODEL), lambda i, j: (i, 0)),
                  pl.BlockSpec((1, D_MODEL), lambda i, j: (0, 0)),
                  pl.BlockSpec((D_MODEL, _IN_TN), lambda i, j: (0, j)),
                  pl.BlockSpec((_IN_TM, LANES), lambda i, j: (i, 0)),
                  pl.BlockSpec((_IN_TM, LANES), lambda i, j: (i, 0))],
        out_specs=pl.BlockSpec((_IN_TM, _IN_TN), lambda i, j: (i, j)),
        out_shape=jax.ShapeDtypeStruct((t, IN_WIDTH), BF16),
        scratch_shapes=[pltpu.VMEM((_IN_TM, D_MODEL), BF16)],
        compiler_params=_cparams(("arbitrary", "arbitrary"), 56),
        name="in_proj",
    )(x2, g, w_bf, cos_t, sin_t)


def _retention_kernel(lg_ref, q_ref, k_ref, v_ref, g_ref, o_ref, state_ref):
    c = RET_CHUNK
    lg = lg_ref[pl.program_id(1)]
    ii = lax.broadcasted_iota(jnp.int32, (c, c), 0)
    jj = lax.broadcasted_iota(jnp.int32, (c, c), 1)
    diff = (ii - jj).astype(F32)
    decay = jnp.where(diff >= 0.0, jnp.exp(jnp.maximum(diff, 0.0) * lg), 0.0)
    pos = lax.broadcasted_iota(jnp.int32, (c, 1), 0).astype(F32)
    k_w = jnp.exp((c - 1.0 - pos) * lg)
    q_w = jnp.exp((pos + 1.0) * lg)
    chunk_decay = jnp.exp(jnp.full((1, RET_V_DIM), float(c), F32) * lg)
    state_ref[...] = jnp.zeros_like(state_ref)

    def body(n, carry):
        sl = pl.ds(pl.multiple_of(n * c, c), c)
        q = q_ref[sl, :]
        k = k_ref[sl, :]
        v = v_ref[sl, :]
        s = lax.dot_general(q, k, _NT, preferred_element_type=F32) * decay
        o = jnp.dot(s.astype(BF16), v, preferred_element_type=F32)
        state = state_ref[...]
        qs = (q.astype(F32) * q_w).astype(BF16)
        o = o + jnp.dot(qs, state.astype(BF16), preferred_element_type=F32)
        kt = (k.astype(F32) * k_w).T.astype(BF16)
        state_ref[...] = chunk_decay * state + jnp.dot(kt, v, preferred_element_type=F32)
        on = o * lax.rsqrt(jnp.mean(o * o, axis=-1, keepdims=True) + RMS_EPS)
        gg = g_ref[sl, :].astype(F32)
        o_ref[sl, :] = (on * (gg * jax.nn.sigmoid(gg))).astype(BF16)
        return carry

    lax.fori_loop(0, q_ref.shape[0] // c, body, 0)


def _retention(log_gamma, proj, batch, seq):
    qb, vb = OFF_RQ // RET_QK_DIM, OFF_RV // RET_V_DIM
    kb, gb = OFF_RK // RET_QK_DIM, OFF_RG // RET_V_DIM
    return pl.pallas_call(
        _retention_kernel,
        grid_spec=pltpu.PrefetchScalarGridSpec(
            num_scalar_prefetch=1,
            grid=(batch, RET_HEADS),
            in_specs=[pl.BlockSpec((seq, RET_QK_DIM), lambda b, h, lg: (b, qb + h)),
                      pl.BlockSpec((seq, RET_QK_DIM), lambda b, h, lg: (b, kb + h)),
                      pl.BlockSpec((seq, RET_V_DIM), lambda b, h, lg: (b, vb + h)),
                      pl.BlockSpec((seq, RET_V_DIM), lambda b, h, lg: (b, gb + h))],
            out_specs=pl.BlockSpec((seq, RET_V_DIM), lambda b, h, lg: (b, h)),
            scratch_shapes=[pltpu.VMEM((RET_QK_DIM, RET_V_DIM), F32)]),
        out_shape=jax.ShapeDtypeStruct((batch * seq, RET_V_W), BF16),
        compiler_params=_cparams(("arbitrary", "arbitrary"), 32),
        name="retention",
    )(log_gamma, proj, proj, proj, proj)


def _moba_kernel(q_ref, k_ref, v_ref, o_ref):
    L = MOBA_BLOCK
    seq = q_ref.shape[0]
    n_kb = seq // L
    scale = MOBA_HEAD_DIM ** -0.5
    kmean = jnp.concatenate(
        [jnp.mean(k_ref[j * L:(j + 1) * L, :].astype(F32), axis=0, keepdims=True) for j in range(n_kb)]
        + [jnp.zeros((LANES - n_kb, MOBA_HEAD_DIM), F32)], axis=0)
    qi_idx = lax.broadcasted_iota(jnp.int32, (L, L), 0)
    kj_idx = lax.broadcasted_iota(jnp.int32, (L, L), 1)
    causal = kj_idx <= qi_idx
    for i in range(n_kb):
        q = q_ref[i * L:(i + 1) * L, :]
        sel = None
        if i > MOBA_TOPK:
            gate = _split_dot(q.astype(F32), kmean, _NT)
            cols = [gate[:, j:j + 1] for j in range(i)]
            sel = []
            for j in range(i):
                rank = jnp.zeros((L, 1), F32)
                for j2 in range(i):
                    if j2 == j:
                        continue
                    ahead = (cols[j2] > cols[j]) | ((cols[j2] == cols[j]) & (j2 < j))
                    rank = rank + ahead.astype(F32)
                sel.append(rank < float(MOBA_TOPK))
        s_blocks = []
        m = None
        for j in range(i + 1):
            s = lax.dot_general(q, k_ref[j * L:(j + 1) * L, :], _NT, preferred_element_type=F32) * scale
            if j == i:
                s = jnp.where(causal, s, NEG_INF)
            elif sel is not None:
                s = jnp.where(sel[j], s, NEG_INF)
            s_blocks.append(s)
            mj = jnp.max(s, axis=-1, keepdims=True)
            m = mj if m is None else jnp.maximum(m, mj)
        acc = jnp.zeros((L, MOBA_HEAD_DIM), F32)
        denom = jnp.zeros((L, 1), F32)
        for j in range(i + 1):
            p = jnp.exp(s_blocks[j] - m)
            denom = denom + jnp.sum(p, axis=-1, keepdims=True)
            acc = acc + jnp.dot(p.astype(BF16), v_ref[j * L:(j + 1) * L, :], preferred_element_type=F32)
        o_ref[i * L:(i + 1) * L, :] = (acc / denom).astype(BF16)


def _moba(proj, batch, seq):
    qb, kb, vb = OFF_MQ // MOBA_HEAD_DIM, OFF_MK // MOBA_HEAD_DIM, OFF_MV // MOBA_HEAD_DIM
    return pl.pallas_call(
        _moba_kernel,
        grid=(batch, MOBA_HEADS),
        in_specs=[pl.BlockSpec((seq, MOBA_HEAD_DIM), lambda b, h: (b, qb + h)),
                  pl.BlockSpec((seq, MOBA_HEAD_DIM), lambda b, h: (b, kb + h)),
                  pl.BlockSpec((seq, MOBA_HEAD_DIM), lambda b, h: (b, vb + h))],
        out_specs=pl.BlockSpec((seq, MOBA_HEAD_DIM), lambda b, h: (b, h)),
        out_shape=jax.ShapeDtypeStruct((batch * seq, MOBA_W), BF16),
        compiler_params=_cparams(("arbitrary", "arbitrary"), 32),
        name="moba",
    )(proj, proj, proj)


_OUT_TM = 256


def _out_proj_kernel(oret_ref, omoba_ref, g1_ref, g2_ref, x_ref, wr_ref, wm_ref, wo_ref,
                     gn_ref, rw_ref, rb_ref, x1_ref, h2_ref, lg_ref):
    yr = jnp.dot(oret_ref[...], wr_ref[...], preferred_element_type=F32)
    ym = jnp.dot(omoba_ref[...], wm_ref[...], preferred_element_type=F32)
    merged = g1_ref[...].astype(F32) * yr + g2_ref[...].astype(F32) * ym
    x1 = x_ref[...] + jnp.dot(merged.astype(BF16), wo_ref[...], preferred_element_type=F32)
    x1_ref[...] = x1
    h2 = x1 * lax.rsqrt(jnp.mean(x1 * x1, axis=-1, keepdims=True) + RMS_EPS) * gn_ref[...]
    h2_ref[...] = _pack_halves(h2)
    lg_ref[...] = _split_dot(h2, rw_ref[...], _NN) + rb_ref[...]


def _out_proj(o_ret, o_moba, proj, x2, wr, wm, wo, gn, rw_pad, rb_pad):
    t = x2.shape[0]
    gblk = OFF_GATE // D_MODEL
    row = lambda i: (i, 0)
    const = lambda i: (0, 0)
    wspec = pl.BlockSpec((D_MODEL, D_MODEL), const, pipeline_mode=pl.Buffered(1))
    return pl.pallas_call(
        _out_proj_kernel,
        grid=(t // _OUT_TM,),
        in_specs=[pl.BlockSpec((_OUT_TM, D_MODEL), row),
                  pl.BlockSpec((_OUT_TM, D_MODEL), row),
                  pl.BlockSpec((_OUT_TM, D_MODEL), lambda i: (i, gblk)),
                  pl.BlockSpec((_OUT_TM, D_MODEL), lambda i: (i, gblk + 1)),
                  pl.BlockSpec((_OUT_TM, D_MODEL), row),
                  wspec, wspec, wspec,
                  pl.BlockSpec((1, D_MODEL), const),
                  pl.BlockSpec((D_MODEL, LANES), const),
                  pl.BlockSpec((1, LANES), const)],
        out_specs=[pl.BlockSpec((_OUT_TM, D_MODEL), row),
                   pl.BlockSpec((_OUT_TM, D_MODEL // 2), row),
                   pl.BlockSpec((_OUT_TM, LANES), row)],
        out_shape=[jax.ShapeDtypeStruct((t, D_MODEL), F32),
                   jax.ShapeDtypeStruct((t, D_MODEL // 2), U32),
                   jax.ShapeDtypeStruct((t, LANES), F32)],
        compiler_params=_cparams(("arbitrary",), 56),
        name="out_proj",
    )(o_ret, o_moba, proj, proj, x2, wr, wm, wo, gn, rw_pad, rb_pad)


_RT_CHUNK = 256
_MAX_UNITS = 64


def _lane_scan(x, lane):
    s = 1
    while s < LANES:
        x = x + jnp.where(lane >= s, pltpu.roll(x, s, 1), 0.0)
        s *= 2
    return x


def _routing_kernel(lg_ref, dest_ref, comb_ref, unit_ref, idx_s, pos_s):
    t = lg_ref.shape[0]
    ch = _RT_CHUNK
    n_ch = t // ch
    lane_i = lax.broadcasted_iota(jnp.int32, (ch, LANES), 1)
    lane = lane_i.astype(F32)
    ri = lax.broadcasted_iota(jnp.int32, (ch, ch), 0)
    ci = lax.broadcasted_iota(jnp.int32, (ch, ch), 1)
    tri = (ci < ri).astype(BF16)

    def phase1(c, carry):
        sl = pl.ds(pl.multiple_of(c * ch, ch), ch)
        l = lg_ref[sl, :]
        onehot = jnp.zeros((ch, LANES), F32)
        vals, idxs = [], []
        for _ in range(TOP_K):
            m = jnp.max(l, axis=-1, keepdims=True)
            idx = jnp.min(jnp.where(l == m, lane, float(LANES)), axis=-1, keepdims=True)
            hit = lane == idx
            vals.append(m)
            idxs.append(idx)
            onehot = onehot + hit.astype(F32)
            l = jnp.where(hit, -jnp.inf, l)
        exps = [jnp.exp(v - vals[0]) for v in vals]
        denom = exps[0] + exps[1] + exps[2] + exps[3]
        before = jnp.dot(tri, onehot.astype(BF16), preferred_element_type=F32) + carry
        idx_row = jnp.zeros((ch, LANES), F32)
        pos_row = jnp.zeros((ch, LANES), F32)
        comb_row = jnp.zeros((ch, LANES), F32)
        for k in range(TOP_K):
            pos_k = jnp.sum(jnp.where(lane == idxs[k], before, 0.0), axis=-1, keepdims=True)
            idx_row = jnp.where(lane_i == k, idxs[k], idx_row)
            pos_row = jnp.where(lane_i == k, pos_k, pos_row)
            comb_row = jnp.where(lane_i == k, exps[k] / denom, comb_row)
        idx_s[sl, :] = idx_row
        pos_s[sl, :] = pos_row
        comb_ref[sl, :] = comb_row
        return carry + jnp.sum(onehot, axis=0, keepdims=True)

    counts = lax.fori_loop(0, n_ch, phase1, jnp.zeros((1, LANES), F32))
    lane8 = lax.broadcasted_iota(jnp.int32, (8, LANES), 1)
    counts8 = jnp.broadcast_to(counts, (8, LANES))
    nblk = jnp.floor((counts8 + (ROW_BLOCK - 1.0)) * (1.0 / ROW_BLOCK))
    padded = nblk * float(ROW_BLOCK)
    pstart = _lane_scan(padded, lane8) - padded
    pstart_row = pstart[0:1, :]

    def phase2(c, carry):
        sl = pl.ds(pl.multiple_of(c * ch, ch), ch)
        idx_row = idx_s[sl, :]
        pos_row = pos_s[sl, :]
        dest_row = jnp.zeros((ch, LANES), F32)
        for k in range(TOP_K):
            start_k = jnp.sum(jnp.where(lane == idx_row[:, k:k + 1], pstart_row, 0.0), axis=-1, keepdims=True)
            dest_row = jnp.where(lane_i == k, start_k + pos_row[:, k:k + 1], dest_row)
        dest_ref[sl, :] = dest_row.astype(jnp.int32)
        return carry

    lax.fori_loop(0, n_ch, phase2, 0)

    units_e = jnp.floor((nblk + (UNIT_BLOCKS - 0.5)) * (1.0 / UNIT_BLOCKS))
    ucum = _lane_scan(units_e, lane8)
    ucum_row = ucum[0:1, :]
    uexcl_row = ucum_row - units_e[0:1, :]
    nblk_row = nblk[0:1, :]
    pblk_row = pstart_row * (1.0 / ROW_BLOCK)
    lane_u = lax.broadcasted_iota(jnp.int32, (_MAX_UNITS, LANES), 1)
    lane_uf = lane_u.astype(F32)
    uu = lax.broadcasted_iota(jnp.int32, (_MAX_UNITS, LANES), 0).astype(F32)
    e_u = jnp.sum(jnp.where((ucum_row <= uu) & (lane_u < N_EXPERTS), 1.0, 0.0), axis=-1, keepdims=True)
    e_u = jnp.minimum(e_u, N_EXPERTS - 1.0)
    pick = lane_uf == e_u
    take = lambda row: jnp.sum(jnp.where(pick, row, 0.0), axis=-1, keepdims=True)
    k_in_e = uu[:, 0:1] - take(uexcl_row)
    nb_u = jnp.clip(take(nblk_row) - k_in_e * UNIT_BLOCKS, 0.0, float(UNIT_BLOCKS))
    rb0_u = take(pblk_row) + k_in_e * UNIT_BLOCKS
    total_blk = jnp.sum(jnp.where(lane_u < N_EXPERTS, nblk_row, 0.0), axis=-1, keepdims=True)
    table = jnp.where(lane_u == 0, e_u, jnp.where(lane_u == 1, rb0_u, jnp.where(lane_u == 2, nb_u,
                      jnp.where(lane_u == 3, total_blk, 0.0))))
    unit_ref[...] = table.astype(jnp.int32)


def _routing(logits):
    t = logits.shape[0]
    return pl.pallas_call(
        _routing_kernel,
        out_shape=[jax.ShapeDtypeStruct((t, LANES), jnp.int32),
                   jax.ShapeDtypeStruct((t, LANES), F32),
                   jax.ShapeDtypeStruct((_MAX_UNITS, LANES), jnp.int32)],
        scratch_shapes=[pltpu.VMEM((t, LANES), F32), pltpu.VMEM((t, LANES), F32)],
        compiler_params=pltpu.CompilerParams(vmem_limit_bytes=48 * MIB),
        name="routing",
    )(logits)


_DISP_TOK = 256


def _dispatch_kernel(dest_ref, h2_ref, zero_hbm, xr_hbm, sem):
    del zero_hbm

    def issue(tok, carry):
        for k in range(TOP_K):
            pltpu.make_async_copy(h2_ref.at[pl.ds(tok, 1)], xr_hbm.at[pl.ds(dest_ref[tok * TOP_K + k], 1)],
                                  sem).start(priority=k % 2)
        return carry

    lax.fori_loop(0, _DISP_TOK, issue, 0, unroll=2)
    for _ in range(TOP_K):
        pltpu.make_async_copy(h2_ref, xr_hbm.at[pl.ds(0, _DISP_TOK)], sem).wait()


def _dispatch(dest_flat, h2p, n_rows):
    t, w = h2p.shape
    zeros = jnp.zeros((n_rows, w), U32)
    return pl.pallas_call(
        _dispatch_kernel,
        grid=(t // _DISP_TOK,),
        in_specs=[pl.BlockSpec((_DISP_TOK * TOP_K,), lambda i: (i,), memory_space=pltpu.SMEM),
                  pl.BlockSpec((_DISP_TOK, w), lambda i: (i, 0)),
                  pl.BlockSpec(memory_space=pl.ANY)],
        out_specs=pl.BlockSpec(memory_space=pl.ANY),
        out_shape=jax.ShapeDtypeStruct((n_rows, w), U32),
        scratch_shapes=[pltpu.SemaphoreType.DMA(())],
        input_output_aliases={2: 0},
        compiler_params=_cparams(("arbitrary",), 16),
        name="dispatch",
    )(dest_flat, h2p, zeros)


_EXP_TF = 256
_EXP_J = D_FF // _EXP_TF
_EXP_TFB = 512
_EXP_JB = D_MODEL // _EXP_TFB
_EXP_STEPS = _EXP_J + _EXP_JB


def _blk_copy(src, s, dst, d, sem):
    return pltpu.make_async_copy(src.at[pl.ds(s, ROW_BLOCK)], dst.at[pl.ds(d, ROW_BLOCK)], sem)


def _experts_kernel(ue_ref, urb_ref, unb_ref, utot_ref, xr_hbm, wg_ref, wu_ref, bg_ref, bu_ref, wd_ref, bd_ref,
                    y_hbm, xf_ref, xb_ref, act_ref, out_ref, wg_bf, wu_bf, wd_bf, sem_in, sem_out):
    u = pl.program_id(0)
    s = pl.program_id(1)
    nb = unb_ref[u]
    half = D_MODEL // 2
    tf = _EXP_TF
    tfb = _EXP_TFB
    last_step = _EXP_STEPS - 1

    def load_x(unit, start):
        def body(r, c):
            cp = _blk_copy(xr_hbm, (urb_ref[unit] + r) * ROW_BLOCK, xf_ref, r * ROW_BLOCK, sem_in)
            cp.start() if start else cp.wait()
            return c

        lax.fori_loop(0, unb_ref[unit], body, 0)

    def store_y(unit, start):
        def body(r, c):
            cp = _blk_copy(out_ref, r * ROW_BLOCK, y_hbm, (urb_ref[unit] + r) * ROW_BLOCK, sem_out)
            cp.start() if start else cp.wait()
            return c

        lax.fori_loop(0, unb_ref[unit], body, 0)

    prev = jnp.maximum(u - 1, 0)
    nxt = jnp.minimum(u + 1, _MAX_UNITS - 1)
    prev_pending = (u > 0) & (unb_ref[prev] > 0)

    @pl.when((u == 0) & (s == 0))
    def _():
        xb_ref[...] = jnp.zeros_like(xb_ref)
        load_x(u, True)

    @pl.when(prev_pending & (((nb > 0) & (s == _EXP_J)) | ((nb == 0) & (s == 0))))
    def _():
        store_y(prev, False)

    @pl.when(nb > 0)
    def _():
        @pl.when(s == 0)
        def _():
            load_x(u, False)

            def cast(r, c):
                sl = pl.ds(pl.multiple_of(r * ROW_BLOCK, ROW_BLOCK), ROW_BLOCK)
                hi, lo = _unpack_halves(xf_ref[sl, :])
                xb_ref[sl, :half] = hi
                xb_ref[sl, half:] = lo
                return c

            lax.fori_loop(0, nb, cast, 0)

        @pl.when(s < _EXP_J)
        def _():
            wg_bf[...] = wg_ref[...].astype(BF16)
            wu_bf[...] = wu_ref[...].astype(BF16)
            x = xb_ref[...]
            g = jnp.dot(x, wg_bf[...], preferred_element_type=F32) + bg_ref[...]
            up = jnp.dot(x, wu_bf[...], preferred_element_type=F32) + bu_ref[...]
            gate = jnp.minimum(g, SWIGLU_LIMIT)
            up = jnp.clip(up, -SWIGLU_LIMIT, SWIGLU_LIMIT)
            act = (up + 1.0) * gate * jax.nn.sigmoid(SWIGLU_ALPHA * gate)
            act_ref[:, pl.ds(pl.multiple_of(s * tf, tf), tf)] = act.astype(BF16)

        @pl.when((s == _EXP_J) & (u + 1 < _MAX_UNITS))
        def _():
            load_x(nxt, True)

        @pl.when(s >= _EXP_J)
        def _():
            wd_bf[...] = wd_ref[...].astype(BF16)
            y = jnp.dot(act_ref[...], wd_bf[...], preferred_element_type=F32) + bd_ref[...]
            out_ref[:, pl.ds(pl.multiple_of((s - _EXP_J) * tfb, tfb), tfb)] = y

        @pl.when(s == last_step)
        def _():
            store_y(u, True)

    @pl.when((u == _MAX_UNITS - 1) & (s == last_step))
    def _():
        @pl.when(nb > 0)
        def _():
            store_y(u, False)

        out_ref[0:ROW_BLOCK, :] = jnp.zeros((ROW_BLOCK, D_MODEL), F32)
        first = utot_ref[0]
        n_blocks = y_hbm.shape[0] // ROW_BLOCK

        def start(r, c):
            _blk_copy(out_ref, 0, y_hbm, r * ROW_BLOCK, sem_out).start()
            return c

        lax.fori_loop(first, n_blocks, start, 0)

        def finish(r, c):
            _blk_copy(out_ref, 0, y_hbm, 0, sem_out).wait()
            return c

        lax.fori_loop(first, n_blocks, finish, 0)


def _experts(unit_e, unit_rb, unit_nb, unit_tot, x_rows, w_gu, b_gu3, w_down, b_down3):
    n_rows = x_rows.shape[0]
    jlast = _EXP_J - 1

    def ja(u, s, nb):
        return jnp.where(nb[u] > 0, jnp.minimum(s, jlast), jlast)

    def jb(u, s, nb):
        return jnp.where(nb[u] > 0, jnp.maximum(s - _EXP_J, 0), _EXP_JB - 1)

    tf = _EXP_TF
    tfb = _EXP_TFB
    return pl.pallas_call(
        _experts_kernel,
        grid_spec=pltpu.PrefetchScalarGridSpec(
            num_scalar_prefetch=4,
            grid=(_MAX_UNITS, _EXP_STEPS),
            in_specs=[pl.BlockSpec(memory_space=pl.ANY),
                      pl.BlockSpec((None, D_MODEL, tf), lambda u, s, e, rb, nb, tot: (e[u], 0, ja(u, s, nb))),
                      pl.BlockSpec((None, D_MODEL, tf),
                                   lambda u, s, e, rb, nb, tot: (e[u], 0, _EXP_J + ja(u, s, nb))),
                      pl.BlockSpec((None, 1, tf), lambda u, s, e, rb, nb, tot: (e[u], 0, ja(u, s, nb))),
                      pl.BlockSpec((None, 1, tf), lambda u, s, e, rb, nb, tot: (e[u], 0, _EXP_J + ja(u, s, nb))),
                      pl.BlockSpec((None, D_FF, tfb), lambda u, s, e, rb, nb, tot: (e[u], 0, jb(u, s, nb))),
                      pl.BlockSpec((None, 1, tfb), lambda u, s, e, rb, nb, tot: (e[u], 0, jb(u, s, nb)))],
            out_specs=pl.BlockSpec(memory_space=pl.ANY),
            scratch_shapes=[pltpu.VMEM((UNIT_ROWS, D_MODEL // 2), U32),
                            pltpu.VMEM((UNIT_ROWS, D_MODEL), BF16),
                            pltpu.VMEM((UNIT_ROWS, D_FF), BF16),
                            pltpu.VMEM((UNIT_ROWS, D_MODEL), F32),
                            pltpu.VMEM((D_MODEL, tf), BF16),
                            pltpu.VMEM((D_MODEL, tf), BF16),
                            pltpu.VMEM((D_FF, tfb), BF16),
                            pltpu.SemaphoreType.DMA(()),
                            pltpu.SemaphoreType.DMA(())]),
        out_shape=jax.ShapeDtypeStruct((n_rows, D_MODEL), F32),
        compiler_params=_cparams(("arbitrary", "arbitrary"), 58),
        name="experts",
    )(unit_e, unit_rb, unit_nb, unit_tot, x_rows, w_gu, w_gu, b_gu3, b_gu3, w_down, b_down3)


_CMB_TOK = 256


def _combine_kernel(dest_ref, dest_next_ref, x1_ref, comb_ref, gf_ref, y_hbm, o_ref, ybuf, sem):
    i = pl.program_id(0)
    n = pl.num_programs(0)
    slot = i % 2

    def gather(d_ref, into):
        def issue(tok, carry):
            for k in range(TOP_K):
                pltpu.make_async_copy(y_hbm.at[pl.ds(d_ref[tok * TOP_K + k], 1)],
                                      ybuf.at[into, k, pl.ds(tok, 1)], sem.at[into]).start(priority=k % 2)
            return carry

        lax.fori_loop(0, _CMB_TOK, issue, 0, unroll=2)

    @pl.when(i == 0)
    def _():
        gather(dest_ref, 0)

    @pl.when(i + 1 < n)
    def _():
        gather(dest_next_ref, 1 - slot)

    for k in range(TOP_K):
        pltpu.make_async_copy(y_hbm.at[pl.ds(0, _CMB_TOK)], ybuf.at[slot, k], sem.at[slot]).wait()

    comb = comb_ref[...]
    x2 = x1_ref[...]
    for k in range(TOP_K):
        x2 = x2 + comb[:, k:k + 1] * ybuf[slot, k]
    o_ref[...] = x2 * lax.rsqrt(jnp.mean(x2 * x2, axis=-1, keepdims=True) + RMS_EPS) * gf_ref[...]


def _combine(dest_flat, x1, comb, g_final, y_rows):
    t = x1.shape[0]
    n = t // _CMB_TOK
    row = lambda i: (i, 0)
    return pl.pallas_call(
        _combine_kernel,
        grid=(n,),
        in_specs=[pl.BlockSpec((_CMB_TOK * TOP_K,), lambda i: (i,), memory_space=pltpu.SMEM),
                  pl.BlockSpec((_CMB_TOK * TOP_K,), lambda i: (jnp.minimum(i + 1, n - 1),),
                               memory_space=pltpu.SMEM),
                  pl.BlockSpec((_CMB_TOK, D_MODEL), row),
                  pl.BlockSpec((_CMB_TOK, LANES), row),
                  pl.BlockSpec((1, D_MODEL), lambda i: (0, 0)),
                  pl.BlockSpec(memory_space=pl.ANY)],
        out_specs=pl.BlockSpec((_CMB_TOK, D_MODEL), row),
        out_shape=jax.ShapeDtypeStruct((t, D_MODEL), F32),
        scratch_shapes=[pltpu.VMEM((2, TOP_K, _CMB_TOK, D_MODEL), F32), pltpu.SemaphoreType.DMA((2,))],
        compiler_params=_cparams(("arbitrary",), 40),
        name="combine",
    )(dest_flat, dest_flat, x1, comb, g_final, y_rows)


def kernel(x, positions, norm_mix_g, w_in, ret_w_o, moba_w_o, w_out, norm_ffn_g, router_w, router_b,
           exp_w_gu, exp_b_gu, exp_w_down, exp_b_down, norm_final_g):
    batch, seq, d = x.shape
    t = batch * seq
    depth = w_in.shape[0]
    assert depth == 1, "the combine kernel applies the closing norm, so exactly one layer is supported"
    half = MOBA_HEAD_DIM // 2
    inv_freq = ROPE_THETA ** (-jnp.arange(half, dtype=F32) / half)
    inv_full = jnp.concatenate([inv_freq, inv_freq])[None, :]
    log_gamma = jnp.log1p(-jnp.exp2(-5.0 - jnp.arange(RET_HEADS, dtype=F32)))
    cos_t, sin_t = _rope_tables(positions.reshape(t, 1), inv_full)
    n_rows = (-(-(t * TOP_K) // ROW_BLOCK) + N_EXPERTS) * ROW_BLOCK

    x2 = x.reshape(t, d)
    for l in range(depth):
        proj = _in_proj(x2, norm_mix_g[l][None, :], w_in[l].astype(BF16), cos_t, sin_t)
        o_ret = _retention(log_gamma, proj, batch, seq)
        o_moba = _moba(proj, batch, seq)
        rw_pad = jnp.pad(router_w[l], ((0, 0), (0, LANES - N_EXPERTS)))
        rb_pad = jnp.pad(router_b[l], (0, LANES - N_EXPERTS), constant_values=NEG_INF)[None, :]
        x1, h2, logits = _out_proj(o_ret, o_moba, proj, x2, ret_w_o[l].astype(BF16), moba_w_o[l].astype(BF16),
                                   w_out[l].astype(BF16), norm_ffn_g[l][None, :], rw_pad, rb_pad)
        dest, comb, units = _routing(logits)
        dest_flat = dest[:, :TOP_K].reshape(t * TOP_K)
        x_rows = _dispatch(dest_flat, h2, n_rows)
        y_rows = _experts(units[:, 0], units[:, 1], units[:, 2], units[0:1, 3], x_rows, exp_w_gu[l],
                          exp_b_gu[l][:, None, :], exp_w_down[l], exp_b_down[l][:, None, :])
        x2 = _combine(dest_flat, x1, comb, norm_final_g[None, :], y_rows)
    return x2.reshape(batch, seq, d)
```

```python
import functools

import jax
import jax.numpy as jnp
from jax import lax
from jax.experimental import pallas as pl
from jax.experimental.pallas import tpu as pltpu

F32 = jnp.float32
BF16 = jnp.bfloat16
U32 = jnp.uint32

D_MODEL = 2048
RET_HEADS = 8
RET_QK_DIM = 128
RET_V_DIM = 256
RET_CHUNK = 128
MOBA_HEADS = 16
MOBA_HEAD_DIM = 128
MOBA_BLOCK = 256
MOBA_TOPK = 3
ROPE_THETA = 10000.0
NEG_INF = -1e30
N_EXPERTS = 32
TOP_K = 4
D_FF = D_MODEL
SWIGLU_LIMIT = 7.0
SWIGLU_ALPHA = 1.702
RMS_EPS = 1e-5

RET_QK_W = RET_HEADS * RET_QK_DIM
RET_V_W = RET_HEADS * RET_V_DIM
MOBA_W = MOBA_HEADS * MOBA_HEAD_DIM
IN_WIDTH = 2 * RET_QK_W + 2 * RET_V_W + 3 * MOBA_W + 2 * D_MODEL
OFF_RQ = 0
OFF_RK = OFF_RQ + RET_QK_W
OFF_RV = OFF_RK + RET_QK_W
OFF_RG = OFF_RV + RET_V_W
OFF_MQ = OFF_RG + RET_V_W
OFF_MK = OFF_MQ + MOBA_W
OFF_MV = OFF_MK + MOBA_W
OFF_GATE = OFF_MV + MOBA_W

LANES = 128
ROW_BLOCK = 256
UNIT_BLOCKS = 5
UNIT_ROWS = UNIT_BLOCKS * ROW_BLOCK
MIB = 1024 * 1024


def _cparams(sem, vmem_mib):
    return pltpu.CompilerParams(dimension_semantics=sem, vmem_limit_bytes=vmem_mib * MIB)


def _split_dot(a, b, dims):
    ah = a.astype(BF16)
    al = (a - ah.astype(F32)).astype(BF16)
    bh = b.astype(BF16)
    bl = (b - bh.astype(F32)).astype(BF16)
    d = functools.partial(lax.dot_general, dimension_numbers=dims, preferred_element_type=F32)
    return d(ah, bh) + (d(ah, bl) + d(al, bh))


def _pack_halves(x):
    n = x.shape[1] // 2
    bits = lax.bitcast_convert_type(x.astype(BF16).astype(F32), U32)
    return bits[:, :n] | lax.shift_right_logical(bits[:, n:], jnp.uint32(16))


def _unpack_halves(u):
    hi = lax.bitcast_convert_type(u & jnp.uint32(0xFFFF0000), F32).astype(BF16)
    lo = lax.bitcast_convert_type(lax.shift_left(u, jnp.uint32(16)), F32).astype(BF16)
    return hi, lo


_NN = (((1,), (0,)), ((), ()))
_NT = (((1,), (1,)), ((), ()))


def _rope_kernel(pos_ref, inv_ref, cos_ref, sin_ref):
    ang = pos_ref[...].astype(F32) * inv_ref[...]
    lane = lax.broadcasted_iota(jnp.int32, ang.shape, 1)
    cos_ref[...] = jnp.cos(ang)
    sin_ref[...] = jnp.where(lane < LANES // 2, -jnp.sin(ang), jnp.sin(ang))


def _rope_tables(pos_col, inv_full):
    t = pos_col.shape[0]
    tm = 1024
    return pl.pallas_call(
        _rope_kernel,
        grid=(t // tm,),
        in_specs=[pl.BlockSpec((tm, 1), lambda i: (i, 0)),
                  pl.BlockSpec((1, LANES), lambda i: (0, 0))],
        out_specs=[pl.BlockSpec((tm, LANES), lambda i: (i, 0))] * 2,
        out_shape=[jax.ShapeDtypeStruct((t, LANES), F32)] * 2,
        compiler_params=_cparams(("arbitrary",), 16),
        name="rope_tables",
    )(pos_col, inv_full)


_IN_TM = 1024
_IN_TN = 1024


def _in_proj_kernel(x_ref, g_ref, w_ref, cos_ref, sin_ref, o_ref, h_ref):
    j = pl.program_id(1)

    @pl.when(j == 0)
    def _():
        x = x_ref[...]
        ms = jnp.mean(x * x, axis=-1, keepdims=True)
        h_ref[...] = (x * lax.rsqrt(ms + RMS_EPS) * g_ref[...]).astype(BF16)

    acc = jnp.dot(h_ref[...], w_ref[...], preferred_element_type=F32)
    col = j * _IN_TN
    is_rk = (col >= OFF_RK) & (col < OFF_RV)
    is_rot = (col < OFF_RV) | ((col >= OFF_MQ) & (col < OFF_MV))
    is_sig = col >= OFF_GATE

    @pl.when(is_rot)
    def _():
        scale = jnp.where(is_rk, RET_QK_DIM ** -0.5, 1.0).astype(F32)
        cos = cos_ref[...]
        sin = sin_ref[...]
        for hh in range(_IN_TN // LANES):
            blk = acc[:, hh * LANES:(hh + 1) * LANES]
            rot = blk * cos + pltpu.roll(blk, LANES // 2, 1) * sin
            o_ref[:, hh * LANES:(hh + 1) * LANES] = (rot * scale).astype(BF16)

    @pl.when(is_sig)
    def _():
        o_ref[...] = jax.nn.sigmoid(acc).astype(BF16)

    @pl.when(jnp.logical_not(is_rot | is_sig))
    def _():
        o_ref[...] = acc.astype(BF16)


def _in_proj(x2, g, w_bf, cos_t, sin_t):
    t = x2.shape[0]
    return pl.pallas_call(
        _in_proj_kernel,
        grid=(t // _IN_TM, IN_WIDTH // _IN_TN),
        in_specs=[pl.BlockSpec((_IN_TM, D_MODEL), lambda i, j: (i, 0)),
                  pl.BlockSpec((1, D_MODEL), lambda i, j: (0, 0)),
                  pl.BlockSpec((D_MODEL, _IN_TN), lambda i, j: (0, j)),
                  pl.BlockSpec((_IN_TM, LANES), lambda i, j: (i, 0)),
                  pl.BlockSpec((_IN_TM, LANES), lambda i, j: (i, 0))],
        out_specs=pl.BlockSpec((_IN_TM, _IN_TN), lambda i, j: (i, j)),
        out_shape=jax.ShapeDtypeStruct((t, IN_WIDTH), BF16),
        scratch_shapes=[pltpu.VMEM((_IN_TM, D_MODEL), BF16)],
        compiler_params=_cparams(("arbitrary", "arbitrary"), 56),
        name="in_proj",
    )(x2, g, w_bf, cos_t, sin_t)


def _retention_kernel(lg_ref, q_ref, k_ref, v_ref, g_ref, o_ref, state_ref):
    c = RET_CHUNK
    lg = lg_ref[pl.program_id(1)]
    ii = lax.broadcasted_iota(jnp.int32, (c, c), 0)
    jj = lax.broadcasted_iota(jnp.int32, (c, c), 1)
    diff = (ii - jj).astype(F32)
    decay = jnp.where(diff >= 0.0, jnp.exp(jnp.maximum(diff, 0.0) * lg), 0.0)
    pos = lax.broadcasted_iota(jnp.int32, (c, 1), 0).astype(F32)
    k_w = jnp.exp((c - 1.0 - pos) * lg)
    q_w = jnp.exp((pos + 1.0) * lg)
    chunk_decay = jnp.exp(jnp.full((1, RET_V_DIM), float(c), F32) * lg)
    state_ref[...] = jnp.zeros_like(state_ref)

    def body(n, carry):
        sl = pl.ds(pl.multiple_of(n * c, c), c)
        q = q_ref[sl, :]
        k = k_ref[sl, :]
        v = v_ref[sl, :]
        s = lax.dot_general(q, k, _NT, preferred_element_type=F32) * decay
        o = jnp.dot(s.astype(BF16), v, preferred_element_type=F32)
        state = state_ref[...]
        qs = (q.astype(F32) * q_w).astype(BF16)
        o = o + jnp.dot(qs, state.astype(BF16), preferred_element_type=F32)
        kt = (k.astype(F32) * k_w).T.astype(BF16)
        state_ref[...] = chunk_decay * state + jnp.dot(kt, v, preferred_element_type=F32)
        on = o * lax.rsqrt(jnp.mean(o * o, axis=-1, keepdims=True) + RMS_EPS)
        gg = g_ref[sl, :].astype(F32)
        o_ref[sl, :] = (on * (gg * jax.nn.sigmoid(gg))).astype(BF16)
        return carry

    lax.fori_loop(0, q_ref.shape[0] // c, body, 0, unroll=2)


def _retention(log_gamma, proj, batch, seq):
    qb, vb = OFF_RQ // RET_QK_DIM, OFF_RV // RET_V_DIM
    kb, gb = OFF_RK // RET_QK_DIM, OFF_RG // RET_V_DIM
    return pl.pallas_call(
        _retention_kernel,
        grid_spec=pltpu.PrefetchScalarGridSpec(
            num_scalar_prefetch=1,
            grid=(batch, RET_HEADS),
            in_specs=[pl.BlockSpec((seq, RET_QK_DIM), lambda b, h, lg: (b, qb + h)),
                      pl.BlockSpec((seq, RET_QK_DIM), lambda b, h, lg: (b, kb + h)),
                      pl.BlockSpec((seq, RET_V_DIM), lambda b, h, lg: (b, vb + h)),
                      pl.BlockSpec((seq, RET_V_DIM), lambda b, h, lg: (b, gb + h))],
            out_specs=pl.BlockSpec((seq, RET_V_DIM), lambda b, h, lg: (b, h)),
            scratch_shapes=[pltpu.VMEM((RET_QK_DIM, RET_V_DIM), F32)]),
        out_shape=jax.ShapeDtypeStruct((batch * seq, RET_V_W), BF16),
        compiler_params=_cparams(("arbitrary", "arbitrary"), 32),
        name="retention",
    )(log_gamma, proj, proj, proj, proj)


def _moba_kernel(q_ref, k_ref, v_ref, o_ref):
    L = MOBA_BLOCK
    seq = q_ref.shape[0]
    n_kb = seq // L
    scale = MOBA_HEAD_DIM ** -0.5
    kmean = jnp.concatenate(
        [jnp.mean(k_ref[j * L:(j + 1) * L, :].astype(F32), axis=0, keepdims=True) for j in range(n_kb)]
        + [jnp.zeros((LANES - n_kb, MOBA_HEAD_DIM), F32)], axis=0)
    qi_idx = lax.broadcasted_iota(jnp.int32, (L, L), 0)
    kj_idx = lax.broadcasted_iota(jnp.int32, (L, L), 1)
    causal = kj_idx <= qi_idx
    for i in range(n_kb):
        q = q_ref[i * L:(i + 1) * L, :]
        sel = None
        if i > MOBA_TOPK:
            gate = _split_dot(q.astype(F32), kmean, _NT)
            cols = [gate[:, j:j + 1] for j in range(i)]
            sel = []
            for j in range(i):
                rank = jnp.zeros((L, 1), F32)
                for j2 in range(i):
                    if j2 == j:
                        continue
                    ahead = (cols[j2] > cols[j]) | ((cols[j2] == cols[j]) & (j2 < j))
                    rank = rank + ahead.astype(F32)
                sel.append(rank < float(MOBA_TOPK))
        s_blocks = []
        m = None
        for j in range(i + 1):
            s = lax.dot_general(q, k_ref[j * L:(j + 1) * L, :], _NT, preferred_element_type=F32) * scale
            if j == i:
                s = jnp.where(causal, s, NEG_INF)
            elif sel is not None:
                s = jnp.where(sel[j], s, NEG_INF)
            s_blocks.append(s)
            mj = jnp.max(s, axis=-1, keepdims=True)
            m = mj if m is None else jnp.maximum(m, mj)
        acc = jnp.zeros((L, MOBA_HEAD_DIM), F32)
        denom = jnp.zeros((L, 1), F32)
        for j in range(i + 1):
            p = jnp.exp(s_blocks[j] - m)
            denom = denom + jnp.sum(p, axis=-1, keepdims=True)
            acc = acc + jnp.dot(p.astype(BF16), v_ref[j * L:(j + 1) * L, :], preferred_element_type=F32)
        o_ref[i * L:(i + 1) * L, :] = (acc / denom).astype(BF16)


def _moba(proj, batch, seq):
    qb, kb, vb = OFF_MQ // MOBA_HEAD_DIM, OFF_MK // MOBA_HEAD_DIM, OFF_MV // MOBA_HEAD_DIM
    return pl.pallas_call(
        _moba_kernel,
        grid=(batch, MOBA_HEADS),
        in_specs=[pl.BlockSpec((seq, MOBA_HEAD_DIM), lambda b, h: (b, qb + h)),
                  pl.BlockSpec((seq, MOBA_HEAD_DIM), lambda b, h: (b, kb + h)),
                  pl.BlockSpec((seq, MOBA_HEAD_DIM), lambda b, h: (b, vb + h))],
        out_specs=pl.BlockSpec((seq, MOBA_HEAD_DIM), lambda b, h: (b, h)),
        out_shape=jax.ShapeDtypeStruct((batch * seq, MOBA_W), BF16),
        compiler_params=_cparams(("arbitrary", "arbitrary"), 32),
        name="moba",
    )(proj, proj, proj)


_OUT_TM = 256


def _out_proj_kernel(oret_ref, omoba_ref, g1_ref, g2_ref, x_ref, wr_ref, wm_ref, wo_ref,
                     gn_ref, rw_ref, rb_ref, x1_ref, h2_ref, lg_ref):
    yr = jnp.dot(oret_ref[...], wr_ref[...], preferred_element_type=F32)
    ym = jnp.dot(omoba_ref[...], wm_ref[...], preferred_element_type=F32)
    merged = g1_ref[...].astype(F32) * yr + g2_ref[...].astype(F32) * ym
    x1 = x_ref[...] + jnp.dot(merged.astype(BF16), wo_ref[...], preferred_element_type=F32)
    x1_ref[...] = x1
    h2 = x1 * lax.rsqrt(jnp.mean(x1 * x1, axis=-1, keepdims=True) + RMS_EPS) * gn_ref[...]
    h2_ref[...] = _pack_halves(h2)
    lg_ref[...] = _split_dot(h2, rw_ref[...], _NN) + rb_ref[...]


def _out_proj(o_ret, o_moba, proj, x2, wr, wm, wo, gn, rw_pad, rb_pad):
    t = x2.shape[0]
    gblk = OFF_GATE // D_MODEL
    row = lambda i: (i, 0)
    const = lambda i: (0, 0)
    wspec = pl.BlockSpec((D_MODEL, D_MODEL), const, pipeline_mode=pl.Buffered(1))
    return pl.pallas_call(
        _out_proj_kernel,
        grid=(t // _OUT_TM,),
        in_specs=[pl.BlockSpec((_OUT_TM, D_MODEL), row),
                  pl.BlockSpec((_OUT_TM, D_MODEL), row),
                  pl.BlockSpec((_OUT_TM, D_MODEL), lambda i: (i, gblk)),
                  pl.BlockSpec((_OUT_TM, D_MODEL), lambda i: (i, gblk + 1)),
                  pl.BlockSpec((_OUT_TM, D_MODEL), row),
                  wspec, wspec, wspec,
                  pl.BlockSpec((1, D_MODEL), const),
                  pl.BlockSpec((D_MODEL, LANES), const),
                  pl.BlockSpec((1, LANES), const)],
        out_specs=[pl.BlockSpec((_OUT_TM, D_MODEL), row),
                   pl.BlockSpec((_OUT_TM, D_MODEL // 2), row),
                   pl.BlockSpec((_OUT_TM, LANES), row)],
        out_shape=[jax.ShapeDtypeStruct((t, D_MODEL), F32),
                   jax.ShapeDtypeStruct((t, D_MODEL // 2), U32),
                   jax.ShapeDtypeStruct((t, LANES), F32)],
        compiler_params=_cparams(("arbitrary",), 56),
        name="out_proj",
    )(o_ret, o_moba, proj, proj, x2, wr, wm, wo, gn, rw_pad, rb_pad)


_RT_CHUNK = 256
_MAX_UNITS = 64


def _lane_scan(x, lane):
    s = 1
    while s < LANES:
        x = x + jnp.where(lane >= s, pltpu.roll(x, s, 1), 0.0)
        s *= 2
    return x


def _routing_kernel(lg_ref, dest_ref, comb_ref, unit_ref, idx_s, pos_s):
    t = lg_ref.shape[0]
    ch = _RT_CHUNK
    n_ch = t // ch
    lane_i = lax.broadcasted_iota(jnp.int32, (ch, LANES), 1)
    lane = lane_i.astype(F32)
    ri = lax.broadcasted_iota(jnp.int32, (ch, ch), 0)
    ci = lax.broadcasted_iota(jnp.int32, (ch, ch), 1)
    tri = (ci < ri).astype(BF16)

    def phase1(c, carry):
        sl = pl.ds(pl.multiple_of(c * ch, ch), ch)
        l = lg_ref[sl, :]
        onehot = jnp.zeros((ch, LANES), F32)
        vals, idxs = [], []
        for _ in range(TOP_K):
            m = jnp.max(l, axis=-1, keepdims=True)
            idx = jnp.min(jnp.where(l == m, lane, float(LANES)), axis=-1, keepdims=True)
            hit = lane == idx
            vals.append(m)
            idxs.append(idx)
            onehot = onehot + hit.astype(F32)
            l = jnp.where(hit, -jnp.inf, l)
        exps = [jnp.exp(v - vals[0]) for v in vals]
        denom = exps[0] + exps[1] + exps[2] + exps[3]
        before = jnp.dot(tri, onehot.astype(BF16), preferred_element_type=F32) + carry
        idx_row = jnp.zeros((ch, LANES), F32)
        pos_row = jnp.zeros((ch, LANES), F32)
        comb_row = jnp.zeros((ch, LANES), F32)
        for k in range(TOP_K):
            pos_k = jnp.sum(jnp.where(lane == idxs[k], before, 0.0), axis=-1, keepdims=True)
            idx_row = jnp.where(lane_i == k, idxs[k], idx_row)
            pos_row = jnp.where(lane_i == k, pos_k, pos_row)
            comb_row = jnp.where(lane_i == k, exps[k] / denom, comb_row)
        idx_s[sl, :] = idx_row
        pos_s[sl, :] = pos_row
        comb_ref[sl, :] = comb_row
        return carry + jnp.sum(onehot, axis=0, keepdims=True)

    counts = lax.fori_loop(0, n_ch, phase1, jnp.zeros((1, LANES), F32))
    lane8 = lax.broadcasted_iota(jnp.int32, (8, LANES), 1)
    counts8 = jnp.broadcast_to(counts, (8, LANES))
    nblk = jnp.floor((counts8 + (ROW_BLOCK - 1.0)) * (1.0 / ROW_BLOCK))
    padded = nblk * float(ROW_BLOCK)
    pstart = _lane_scan(padded, lane8) - padded
    pstart_row = pstart[0:1, :]

    def phase2(c, carry):
        sl = pl.ds(pl.multiple_of(c * ch, ch), ch)
        idx_row = idx_s[sl, :]
        pos_row = pos_s[sl, :]
        dest_row = jnp.zeros((ch, LANES), F32)
        for k in range(TOP_K):
            start_k = jnp.sum(jnp.where(lane == idx_row[:, k:k + 1], pstart_row, 0.0), axis=-1, keepdims=True)
            dest_row = jnp.where(lane_i == k, start_k + pos_row[:, k:k + 1], dest_row)
        dest_ref[sl, :] = dest_row.astype(jnp.int32)
        return carry

    lax.fori_loop(0, n_ch, phase2, 0)

    units_e = jnp.floor((nblk + (UNIT_BLOCKS - 0.5)) * (1.0 / UNIT_BLOCKS))
    ucum = _lane_scan(units_e, lane8)
    ucum_row = ucum[0:1, :]
    uexcl_row = ucum_row - units_e[0:1, :]
    nblk_row = nblk[0:1, :]
    pblk_row = pstart_row * (1.0 / ROW_BLOCK)
    lane_u = lax.broadcasted_iota(jnp.int32, (_MAX_UNITS, LANES), 1)
    lane_uf = lane_u.astype(F32)
    uu = lax.broadcasted_iota(jnp.int32, (_MAX_UNITS, LANES), 0).astype(F32)
    e_u = jnp.sum(jnp.where((ucum_row <= uu) & (lane_u < N_EXPERTS), 1.0, 0.0), axis=-1, keepdims=True)
    e_u = jnp.minimum(e_u, N_EXPERTS - 1.0)
    pick = lane_uf == e_u
    take = lambda row: jnp.sum(jnp.where(pick, row, 0.0), axis=-1, keepdims=True)
    k_in_e = uu[:, 0:1] - take(uexcl_row)
    nb_u = jnp.clip(take(nblk_row) - k_in_e * UNIT_BLOCKS, 0.0, float(UNIT_BLOCKS))
    rb0_u = take(pblk_row) + k_in_e * UNIT_BLOCKS
    total_blk = jnp.sum(jnp.where(lane_u < N_EXPERTS, nblk_row, 0.0), axis=-1, keepdims=True)
    table = jnp.where(lane_u == 0, e_u, jnp.where(lane_u == 1, rb0_u, jnp.where(lane_u == 2, nb_u,
                      jnp.where(lane_u == 3, total_blk, 0.0))))
    unit_ref[...] = table.astype(jnp.int32)


def _routing(logits):
    t = logits.shape[0]
    return pl.pallas_call(
        _routing_kernel,
        out_shape=[jax.ShapeDtypeStruct((t, LANES), jnp.int32),
                   jax.ShapeDtypeStruct((t, LANES), F32),
                   jax.ShapeDtypeStruct((_MAX_UNITS, LANES), jnp.int32)],
        scratch_shapes=[pltpu.VMEM((t, LANES), F32), pltpu.VMEM((t, LANES), F32)],
        compiler_params=pltpu.CompilerParams(vmem_limit_bytes=48 * MIB),
        name="routing",
    )(logits)


_DISP_TOK = 256


def _dispatch_kernel(dest_ref, h2_ref, zero_hbm, xr_hbm, sem):
    del zero_hbm

    def issue(tok, carry):
        for k in range(TOP_K):
            pltpu.make_async_copy(h2_ref.at[pl.ds(tok, 1)], xr_hbm.at[pl.ds(dest_ref[tok * TOP_K + k], 1)],
                                  sem).start(priority=k % 2)
        return carry

    lax.fori_loop(0, _DISP_TOK, issue, 0, unroll=2)
    for _ in range(TOP_K):
        pltpu.make_async_copy(h2_ref, xr_hbm.at[pl.ds(0, _DISP_TOK)], sem).wait()


def _dispatch(dest_flat, h2p, n_rows):
    t, w = h2p.shape
    zeros = jnp.zeros((n_rows, w), U32)
    return pl.pallas_call(
        _dispatch_kernel,
        grid=(t // _DISP_TOK,),
        in_specs=[pl.BlockSpec((_DISP_TOK * TOP_K,), lambda i: (i,), memory_space=pltpu.SMEM),
                  pl.BlockSpec((_DISP_TOK, w), lambda i: (i, 0)),
                  pl.BlockSpec(memory_space=pl.ANY)],
        out_specs=pl.BlockSpec(memory_space=pl.ANY),
        out_shape=jax.ShapeDtypeStruct((n_rows, w), U32),
        scratch_shapes=[pltpu.SemaphoreType.DMA(())],
        input_output_aliases={2: 0},
        compiler_params=_cparams(("arbitrary",), 16),
        name="dispatch",
    )(dest_flat, h2p, zeros)


_EXP_TF = 256
_EXP_J = D_FF // _EXP_TF
_EXP_TFB = 512
_EXP_JB = D_MODEL // _EXP_TFB
_EXP_STEPS = _EXP_J + _EXP_JB


def _blk_copy(src, s, dst, d, sem):
    return pltpu.make_async_copy(src.at[pl.ds(s, ROW_BLOCK)], dst.at[pl.ds(d, ROW_BLOCK)], sem)


def _experts_kernel(ue_ref, urb_ref, unb_ref, utot_ref, xr_hbm, wg_ref, wu_ref, bg_ref, bu_ref, wd_ref, bd_ref,
                    y_hbm, xf_ref, xb_ref, act_ref, out_ref, wg_bf, wu_bf, wd_bf, sem_in, sem_out):
    u = pl.program_id(0)
    s = pl.program_id(1)
    nb = unb_ref[u]
    half = D_MODEL // 2
    tf = _EXP_TF
    tfb = _EXP_TFB
    last_step = _EXP_STEPS - 1

    def load_x(unit, start):
        def body(r, c):
            cp = _blk_copy(xr_hbm, (urb_ref[unit] + r) * ROW_BLOCK, xf_ref, r * ROW_BLOCK, sem_in)
            cp.start() if start else cp.wait()
            return c

        lax.fori_loop(0, unb_ref[unit], body, 0)

    def store_y(unit, start):
        def body(r, c):
            cp = _blk_copy(out_ref, r * ROW_BLOCK, y_hbm, (urb_ref[unit] + r) * ROW_BLOCK, sem_out)
            cp.start() if start else cp.wait()
            return c

        lax.fori_loop(0, unb_ref[unit], body, 0)

    prev = jnp.maximum(u - 1, 0)
    nxt = jnp.minimum(u + 1, _MAX_UNITS - 1)
    prev_pending = (u > 0) & (unb_ref[prev] > 0)

    @pl.when((u == 0) & (s == 0))
    def _():
        xb_ref[...] = jnp.zeros_like(xb_ref)
        load_x(u, True)

    @pl.when(prev_pending & (((nb > 0) & (s == _EXP_J)) | ((nb == 0) & (s == 0))))
    def _():
        store_y(prev, False)

    @pl.when(nb > 0)
    def _():
        @pl.when(s == 0)
        def _():
            load_x(u, False)

            def cast(r, c):
                sl = pl.ds(pl.multiple_of(r * ROW_BLOCK, ROW_BLOCK), ROW_BLOCK)
                hi, lo = _unpack_halves(xf_ref[sl, :])
                xb_ref[sl, :half] = hi
                xb_ref[sl, half:] = lo
                return c

            lax.fori_loop(0, nb, cast, 0)

        def activation(rows):
            x = xb_ref[0:rows, :]
            g = jnp.dot(x, wg_bf[...], preferred_element_type=F32) + bg_ref[...]
            up = jnp.dot(x, wu_bf[...], preferred_element_type=F32) + bu_ref[...]
            gate = jnp.minimum(g, SWIGLU_LIMIT)
            up = jnp.clip(up, -SWIGLU_LIMIT, SWIGLU_LIMIT)
            act = (up + 1.0) * gate * jax.nn.sigmoid(SWIGLU_ALPHA * gate)
            act_ref[0:rows, pl.ds(pl.multiple_of(s * tf, tf), tf)] = act.astype(BF16)

        def down(rows):
            y = jnp.dot(act_ref[0:rows, :], wd_bf[...], preferred_element_type=F32) + bd_ref[...]
            out_ref[0:rows, pl.ds(pl.multiple_of((s - _EXP_J) * tfb, tfb), tfb)] = y

        short = nb < UNIT_BLOCKS

        @pl.when(s < _EXP_J)
        def _():
            wg_bf[...] = wg_ref[...].astype(BF16)
            wu_bf[...] = wu_ref[...].astype(BF16)
            pl.when(short)(lambda: activation(UNIT_ROWS - ROW_BLOCK))
            pl.when(jnp.logical_not(short))(lambda: activation(UNIT_ROWS))

        @pl.when((s == _EXP_J) & (u + 1 < _MAX_UNITS))
        def _():
            load_x(nxt, True)

        @pl.when(s >= _EXP_J)
        def _():
            wd_bf[...] = wd_ref[...].astype(BF16)
            pl.when(short)(lambda: down(UNIT_ROWS - ROW_BLOCK))
            pl.when(jnp.logical_not(short))(lambda: down(UNIT_ROWS))

        @pl.when(s == last_step)
        def _():
            store_y(u, True)

    @pl.when((u == _MAX_UNITS - 1) & (s == last_step))
    def _():
        @pl.when(nb > 0)
        def _():
            store_y(u, False)

        out_ref[0:ROW_BLOCK, :] = jnp.zeros((ROW_BLOCK, D_MODEL), F32)
        first = utot_ref[0]
        n_blocks = y_hbm.shape[0] // ROW_BLOCK

        def start(r, c):
            _blk_copy(out_ref, 0, y_hbm, r * ROW_BLOCK, sem_out).start()
            return c

        lax.fori_loop(first, n_blocks, start, 0)

        def finish(r, c):
            _blk_copy(out_ref, 0, y_hbm, 0, sem_out).wait()
            return c

        lax.fori_loop(first, n_blocks, finish, 0)


def _experts(unit_e, unit_rb, unit_nb, unit_tot, x_rows, w_gu, b_gu3, w_down, b_down3):
    n_rows = x_rows.shape[0]
    jlast = _EXP_J - 1

    def ja(u, s, nb):
        return jnp.where(nb[u] > 0, jnp.minimum(s, jlast), jlast)

    def jb(u, s, nb):
        return jnp.where(nb[u] > 0, jnp.maximum(s - _EXP_J, 0), _EXP_JB - 1)

    tf = _EXP_TF
    tfb = _EXP_TFB
    return pl.pallas_call(
        _experts_kernel,
        grid_spec=pltpu.PrefetchScalarGridSpec(
            num_scalar_prefetch=4,
            grid=(_MAX_UNITS, _EXP_STEPS),
            in_specs=[pl.BlockSpec(memory_space=pl.ANY),
                      pl.BlockSpec((None, D_MODEL, tf), lambda u, s, e, rb, nb, tot: (e[u], 0, ja(u, s, nb))),
                      pl.BlockSpec((None, D_MODEL, tf),
                                   lambda u, s, e, rb, nb, tot: (e[u], 0, _EXP_J + ja(u, s, nb))),
                      pl.BlockSpec((None, 1, tf), lambda u, s, e, rb, nb, tot: (e[u], 0, ja(u, s, nb))),
                      pl.BlockSpec((None, 1, tf), lambda u, s, e, rb, nb, tot: (e[u], 0, _EXP_J + ja(u, s, nb))),
                      pl.BlockSpec((None, D_FF, tfb), lambda u, s, e, rb, nb, tot: (e[u], 0, jb(u, s, nb))),
                      pl.BlockSpec((None, 1, tfb), lambda u, s, e, rb, nb, tot: (e[u], 0, jb(u, s, nb)))],
            out_specs=pl.BlockSpec(memory_space=pl.ANY),
            scratch_shapes=[pltpu.VMEM((UNIT_ROWS, D_MODEL // 2), U32),
                            pltpu.VMEM((UNIT_ROWS, D_MODEL), BF16),
                            pltpu.VMEM((UNIT_ROWS, D_FF), BF16),
                            pltpu.VMEM((UNIT_ROWS, D_MODEL), F32),
                            pltpu.VMEM((D_MODEL, tf), BF16),
                            pltpu.VMEM((D_MODEL, tf), BF16),
                            pltpu.VMEM((D_FF, tfb), BF16),
                            pltpu.SemaphoreType.DMA(()),
                            pltpu.SemaphoreType.DMA(())]),
        out_shape=jax.ShapeDtypeStruct((n_rows, D_MODEL), F32),
        compiler_params=_cparams(("arbitrary", "arbitrary"), 58),
        name="experts",
    )(unit_e, unit_rb, unit_nb, unit_tot, x_rows, w_gu, w_gu, b_gu3, b_gu3, w_down, b_down3)


_CMB_TOK = 256


def _combine_kernel(dest_ref, dest_next_ref, x1_ref, comb_ref, gf_ref, y_hbm, o_ref, ybuf, sem):
    i = pl.program_id(0)
    n = pl.num_programs(0)
    slot = i % 2

    def gather(d_ref, into):
        def issue(tok, carry):
            for k in range(TOP_K):
                pltpu.make_async_copy(y_hbm.at[pl.ds(d_ref[tok * TOP_K + k], 1)],
                                      ybuf.at[into, k, pl.ds(tok, 1)], sem.at[into]).start(priority=k % 2)
            return carry

        lax.fori_loop(0, _CMB_TOK, issue, 0, unroll=2)

    @pl.when(i == 0)
    def _():
        gather(dest_ref, 0)

    @pl.when(i + 1 < n)
    def _():
        gather(dest_next_ref, 1 - slot)

    for k in range(TOP_K):
        pltpu.make_async_copy(y_hbm.at[pl.ds(0, _CMB_TOK)], ybuf.at[slot, k], sem.at[slot]).wait()

    comb = comb_ref[...]
    x2 = x1_ref[...]
    for k in range(TOP_K):
        x2 = x2 + comb[:, k:k + 1] * ybuf[slot, k]
    o_ref[...] = x2 * lax.rsqrt(jnp.mean(x2 * x2, axis=-1, keepdims=True) + RMS_EPS) * gf_ref[...]


def _combine(dest_flat, x1, comb, g_final, y_rows):
    t = x1.shape[0]
    n = t // _CMB_TOK
    row = lambda i: (i, 0)
    return pl.pallas_call(
        _combine_kernel,
        grid=(n,),
        in_specs=[pl.BlockSpec((_CMB_TOK * TOP_K,), lambda i: (i,), memory_space=pltpu.SMEM),
                  pl.BlockSpec((_CMB_TOK * TOP_K,), lambda i: (jnp.minimum(i + 1, n - 1),),
                               memory_space=pltpu.SMEM),
                  pl.BlockSpec((_CMB_TOK, D_MODEL), row),
                  pl.BlockSpec((_CMB_TOK, LANES), row),
                  pl.BlockSpec((1, D_MODEL), lambda i: (0, 0)),
                  pl.BlockSpec(memory_space=pl.ANY)],
        out_specs=pl.BlockSpec((_CMB_TOK, D_MODEL), row),
        out_shape=jax.ShapeDtypeStruct((t, D_MODEL), F32),
        scratch_shapes=[pltpu.VMEM((2, TOP_K, _CMB_TOK, D_MODEL), F32), pltpu.SemaphoreType.DMA((2,))],
        compiler_params=_cparams(("arbitrary",), 40),
        name="combine",
    )(dest_flat, dest_flat, x1, comb, g_final, y_rows)


def kernel(x, positions, norm_mix_g, w_in, ret_w_o, moba_w_o, w_out, norm_ffn_g, router_w, router_b,
           exp_w_gu, exp_b_gu, exp_w_down, exp_b_down, norm_final_g):
    batch, seq, d = x.shape
    t = batch * seq
    depth = w_in.shape[0]
    assert depth == 1, "the combine kernel applies the closing norm, so exactly one layer is supported"
    half = MOBA_HEAD_DIM // 2
    inv_freq = ROPE_THETA ** (-jnp.arange(half, dtype=F32) / half)
    inv_full = jnp.concatenate([inv_freq, inv_freq])[None, :]
    log_gamma = jnp.log1p(-jnp.exp2(-5.0 - jnp.arange(RET_HEADS, dtype=F32)))
    cos_t, sin_t = _rope_tables(positions.reshape(t, 1), inv_full)
    n_rows = (-(-(t * TOP_K) // ROW_BLOCK) + N_EXPERTS) * ROW_BLOCK

    x2 = x.reshape(t, d)
    for l in range(depth):
        proj = _in_proj(x2, norm_mix_g[l][None, :], w_in[l].astype(BF16), cos_t, sin_t)
        o_ret = _retention(log_gamma, proj, batch, seq)
        o_moba = _moba(proj, batch, seq)
        rw_pad = jnp.pad(router_w[l], ((0, 0), (0, LANES - N_EXPERTS)))
        rb_pad = jnp.pad(router_b[l], (0, LANES - N_EXPERTS), constant_values=NEG_INF)[None, :]
        x1, h2, logits = _out_proj(o_ret, o_moba, proj, x2, ret_w_o[l].astype(BF16), moba_w_o[l].astype(BF16),
                                   w_out[l].astype(BF16), norm_ffn_g[l][None, :], rw_pad, rb_pad)
        dest, comb, units = _routing(logits)
        dest_flat = dest[:, :TOP_K].reshape(t * TOP_K)
        x_rows = _dispatch(dest_flat, h2, n_rows)
        y_rows = _experts(units[:, 0], units[:, 1], units[:, 2], units[0:1, 3], x_rows, exp_w_gu[l],
                          exp_b_gu[l][:, None, :], exp_w_down[l], exp_b_down[l][:, None, :])
        x2 = _combine(dest_flat, x1, comb, norm_final_g[None, :], y_rows)
    return x2.reshape(batch, seq, d)
```

```python
import functools

import jax
import jax.numpy as jnp
from jax import lax
from jax.experimental import pallas as pl
from jax.experimental.pallas import tpu as pltpu

F32 = jnp.float32
BF16 = jnp.bfloat16
U32 = jnp.uint32

D_MODEL = 2048
RET_HEADS = 8
RET_QK_DIM = 128
RET_V_DIM = 256
RET_CHUNK = 128
MOBA_HEADS = 16
MOBA_HEAD_DIM = 128
MOBA_BLOCK = 256
MOBA_TOPK = 3
ROPE_THETA = 10000.0
NEG_INF = -1e30
N_EXPERTS = 32
TOP_K = 4
D_FF = D_MODEL
SWIGLU_LIMIT = 7.0
SWIGLU_ALPHA = 1.702
RMS_EPS = 1e-5

RET_QK_W = RET_HEADS * RET_QK_DIM
RET_V_W = RET_HEADS * RET_V_DIM
MOBA_W = MOBA_HEADS * MOBA_HEAD_DIM
IN_WIDTH = 2 * RET_QK_W + 2 * RET_V_W + 3 * MOBA_W + 2 * D_MODEL
OFF_RQ = 0
OFF_RK = OFF_RQ + RET_QK_W
OFF_RV = OFF_RK + RET_QK_W
OFF_RG = OFF_RV + RET_V_W
OFF_MQ = OFF_RG + RET_V_W
OFF_MK = OFF_MQ + MOBA_W
OFF_MV = OFF_MK + MOBA_W
OFF_GATE = OFF_MV + MOBA_W

LANES = 128
ROW_BLOCK = 256
UNIT_BLOCKS = 5
UNIT_ROWS = UNIT_BLOCKS * ROW_BLOCK
MIB = 1024 * 1024


def _cparams(sem, vmem_mib):
    return pltpu.CompilerParams(dimension_semantics=sem, vmem_limit_bytes=vmem_mib * MIB)


def _split_dot(a, b, dims):
    ah = a.astype(BF16)
    al = (a - ah.astype(F32)).astype(BF16)
    bh = b.astype(BF16)
    bl = (b - bh.astype(F32)).astype(BF16)
    d = functools.partial(lax.dot_general, dimension_numbers=dims, preferred_element_type=F32)
    return d(ah, bh) + (d(ah, bl) + d(al, bh))


def _pack_halves(x):
    n = x.shape[1] // 2
    bits = lax.bitcast_convert_type(x.astype(BF16).astype(F32), U32)
    return bits[:, :n] | lax.shift_right_logical(bits[:, n:], jnp.uint32(16))


def _unpack_halves(u):
    hi = lax.bitcast_convert_type(u & jnp.uint32(0xFFFF0000), F32).astype(BF16)
    lo = lax.bitcast_convert_type(lax.shift_left(u, jnp.uint32(16)), F32).astype(BF16)
    return hi, lo


_NN = (((1,), (0,)), ((), ()))
_NT = (((1,), (1,)), ((), ()))


def _rope_kernel(pos_ref, inv_ref, cos_ref, sin_ref):
    ang = pos_ref[...].astype(F32) * inv_ref[...]
    lane = lax.broadcasted_iota(jnp.int32, ang.shape, 1)
    cos_ref[...] = jnp.cos(ang)
    sin_ref[...] = jnp.where(lane < LANES // 2, -jnp.sin(ang), jnp.sin(ang))


def _rope_tables(pos_col, inv_full):
    t = pos_col.shape[0]
    tm = 1024
    return pl.pallas_call(
        _rope_kernel,
        grid=(t // tm,),
        in_specs=[pl.BlockSpec((tm, 1), lambda i: (i, 0)),
                  pl.BlockSpec((1, LANES), lambda i: (0, 0))],
        out_specs=[pl.BlockSpec((tm, LANES), lambda i: (i, 0))] * 2,
        out_shape=[jax.ShapeDtypeStruct((t, LANES), F32)] * 2,
        compiler_params=_cparams(("arbitrary",), 16),
        name="rope_tables",
    )(pos_col, inv_full)


_IN_TM = 1024
_IN_TN = 1024


def _in_proj_kernel(x_ref, g_ref, w_ref, cos_ref, sin_ref, o_ref, h_ref):
    j = pl.program_id(1)

    @pl.when(j == 0)
    def _():
        x = x_ref[...]
        ms = jnp.mean(x * x, axis=-1, keepdims=True)
        h_ref[...] = (x * lax.rsqrt(ms + RMS_EPS) * g_ref[...]).astype(BF16)

    acc = jnp.dot(h_ref[...], w_ref[...], preferred_element_type=F32)
    col = j * _IN_TN
    is_rk = (col >= OFF_RK) & (col < OFF_RV)
    is_rot = (col < OFF_RV) | ((col >= OFF_MQ) & (col < OFF_MV))
    is_sig = col >= OFF_GATE

    @pl.when(is_rot)
    def _():
        scale = jnp.where(is_rk, RET_QK_DIM ** -0.5, 1.0).astype(F32)
        cos = cos_ref[...]
        sin = sin_ref[...]
        for hh in range(_IN_TN // LANES):
            blk = acc[:, hh * LANES:(hh + 1) * LANES]
            rot = blk * cos + pltpu.roll(blk, LANES // 2, 1) * sin
            o_ref[:, hh * LANES:(hh + 1) * LANES] = (rot * scale).astype(BF16)

    @pl.when(is_sig)
    def _():
        o_ref[...] = jax.nn.sigmoid(acc).astype(BF16)

    @pl.when(jnp.logical_not(is_rot | is_sig))
    def _():
        o_ref[...] = acc.astype(BF16)


def _in_proj(x2, g, w_bf, cos_t, sin_t):
    t = x2.shape[0]
    return pl.pallas_call(
        _in_proj_kernel,
        grid=(t // _IN_TM, IN_WIDTH // _IN_TN),
        in_specs=[pl.BlockSpec((_IN_TM, D_MODEL), lambda i, j: (i, 0)),
                  pl.BlockSpec((1, D_MODEL), lambda i, j: (0, 0)),
                  pl.BlockSpec((D_MODEL, _IN_TN), lambda i, j: (0, j)),
                  pl.BlockSpec((_IN_TM, LANES), lambda i, j: (i, 0)),
                  pl.BlockSpec((_IN_TM, LANES), lambda i, j: (i, 0))],
        out_specs=pl.BlockSpec((_IN_TM, _IN_TN), lambda i, j: (i, j)),
        out_shape=jax.ShapeDtypeStruct((t, IN_WIDTH), BF16),
        scratch_shapes=[pltpu.VMEM((_IN_TM, D_MODEL), BF16)],
        compiler_params=_cparams(("arbitrary", "arbitrary"), 56),
        name="in_proj",
    )(x2, g, w_bf, cos_t, sin_t)


def _retention_kernel(lg_ref, q_ref, k_ref, v_ref, g_ref, o_ref, state_ref):
    c = RET_CHUNK
    lg = lg_ref[pl.program_id(1)]
    ii = lax.broadcasted_iota(jnp.int32, (c, c), 0)
    jj = lax.broadcasted_iota(jnp.int32, (c, c), 1)
    diff = (ii - jj).astype(F32)
    decay = jnp.where(diff >= 0.0, jnp.exp(jnp.maximum(diff, 0.0) * lg), 0.0)
    pos = lax.broadcasted_iota(jnp.int32, (c, 1), 0).astype(F32)
    k_w = jnp.exp((c - 1.0 - pos) * lg)
    q_w = jnp.exp((pos + 1.0) * lg)
    chunk_decay = jnp.exp(jnp.full((1, RET_V_DIM), float(c), F32) * lg)
    state_ref[...] = jnp.zeros_like(state_ref)

    def body(n, carry):
        sl = pl.ds(pl.multiple_of(n * c, c), c)
        q = q_ref[sl, :]
        k = k_ref[sl, :]
        v = v_ref[sl, :]
        s = lax.dot_general(q, k, _NT, preferred_element_type=F32) * decay
        o = jnp.dot(s.astype(BF16), v, preferred_element_type=F32)
        state = state_ref[...]
        qs = (q.astype(F32) * q_w).astype(BF16)
        o = o + jnp.dot(qs, state.astype(BF16), preferred_element_type=F32)
        kt = (k.astype(F32) * k_w).T.astype(BF16)
        state_ref[...] = chunk_decay * state + jnp.dot(kt, v, preferred_element_type=F32)
        on = o * lax.rsqrt(jnp.mean(o * o, axis=-1, keepdims=True) + RMS_EPS)
        gg = g_ref[sl, :].astype(F32)
        o_ref[sl, :] = (on * (gg * jax.nn.sigmoid(gg))).astype(BF16)
        return carry

    lax.fori_loop(0, q_ref.shape[0] // c, body, 0, unroll=4)


def _retention(log_gamma, proj, batch, seq):
    qb, vb = OFF_RQ // RET_QK_DIM, OFF_RV // RET_V_DIM
    kb, gb = OFF_RK // RET_QK_DIM, OFF_RG // RET_V_DIM
    return pl.pallas_call(
        _retention_kernel,
        grid_spec=pltpu.PrefetchScalarGridSpec(
            num_scalar_prefetch=1,
            grid=(batch, RET_HEADS),
            in_specs=[pl.BlockSpec((seq, RET_QK_DIM), lambda b, h, lg: (b, qb + h)),
                      pl.BlockSpec((seq, RET_QK_DIM), lambda b, h, lg: (b, kb + h)),
                      pl.BlockSpec((seq, RET_V_DIM), lambda b, h, lg: (b, vb + h)),
                      pl.BlockSpec((seq, RET_V_DIM), lambda b, h, lg: (b, gb + h))],
            out_specs=pl.BlockSpec((seq, RET_V_DIM), lambda b, h, lg: (b, h)),
            scratch_shapes=[pltpu.VMEM((RET_QK_DIM, RET_V_DIM), F32)]),
        out_shape=jax.ShapeDtypeStruct((batch * seq, RET_V_W), BF16),
        compiler_params=_cparams(("arbitrary", "arbitrary"), 32),
        name="retention",
    )(log_gamma, proj, proj, proj, proj)


def _moba_kernel(q_ref, k_ref, v_ref, o_ref):
    L = MOBA_BLOCK
    seq = q_ref.shape[0]
    n_kb = seq // L
    scale = MOBA_HEAD_DIM ** -0.5
    kmean = jnp.concatenate(
        [jnp.mean(k_ref[j * L:(j + 1) * L, :].astype(F32), axis=0, keepdims=True) for j in range(n_kb)]
        + [jnp.zeros((LANES - n_kb, MOBA_HEAD_DIM), F32)], axis=0)
    qi_idx = lax.broadcasted_iota(jnp.int32, (L, L), 0)
    kj_idx = lax.broadcasted_iota(jnp.int32, (L, L), 1)
    causal = kj_idx <= qi_idx
    for i in range(n_kb):
        q = q_ref[i * L:(i + 1) * L, :]
        sel = None
        if i > MOBA_TOPK:
            gate = _split_dot(q.astype(F32), kmean, _NT)
            cols = [gate[:, j:j + 1] for j in range(i)]
            sel = []
            for j in range(i):
                rank = jnp.zeros((L, 1), F32)
                for j2 in range(i):
                    if j2 == j:
                        continue
                    ahead = (cols[j2] > cols[j]) | ((cols[j2] == cols[j]) & (j2 < j))
                    rank = rank + ahead.astype(F32)
                sel.append(rank < float(MOBA_TOPK))
        s_blocks = []
        m = None
        for j in range(i + 1):
            s = lax.dot_general(q, k_ref[j * L:(j + 1) * L, :], _NT, preferred_element_type=F32) * scale
            if j == i:
                s = jnp.where(causal, s, NEG_INF)
            elif sel is not None:
                s = jnp.where(sel[j], s, NEG_INF)
            s_blocks.append(s)
            mj = jnp.max(s, axis=-1, keepdims=True)
            m = mj if m is None else jnp.maximum(m, mj)
        acc = jnp.zeros((L, MOBA_HEAD_DIM), F32)
        denom = jnp.zeros((L, 1), F32)
        for j in range(i + 1):
            p = jnp.exp(s_blocks[j] - m)
            denom = denom + jnp.sum(p, axis=-1, keepdims=True)
            acc = acc + jnp.dot(p.astype(BF16), v_ref[j * L:(j + 1) * L, :], preferred_element_type=F32)
        o_ref[i * L:(i + 1) * L, :] = (acc / denom).astype(BF16)


def _moba(proj, batch, seq):
    qb, kb, vb = OFF_MQ // MOBA_HEAD_DIM, OFF_MK // MOBA_HEAD_DIM, OFF_MV // MOBA_HEAD_DIM
    return pl.pallas_call(
        _moba_kernel,
        grid=(batch, MOBA_HEADS),
        in_specs=[pl.BlockSpec((seq, MOBA_HEAD_DIM), lambda b, h: (b, qb + h)),
                  pl.BlockSpec((seq, MOBA_HEAD_DIM), lambda b, h: (b, kb + h)),
                  pl.BlockSpec((seq, MOBA_HEAD_DIM), lambda b, h: (b, vb + h))],
        out_specs=pl.BlockSpec((seq, MOBA_HEAD_DIM), lambda b, h: (b, h)),
        out_shape=jax.ShapeDtypeStruct((batch * seq, MOBA_W), BF16),
        compiler_params=_cparams(("arbitrary", "arbitrary"), 32),
        name="moba",
    )(proj, proj, proj)


_OUT_TM = 256


def _out_proj_kernel(oret_ref, omoba_ref, g1_ref, g2_ref, x_ref, wr_ref, wm_ref, wo_ref,
                     gn_ref, rw_ref, rb_ref, x1_ref, h2_ref, lg_ref):
    yr = jnp.dot(oret_ref[...], wr_ref[...], preferred_element_type=F32)
    ym = jnp.dot(omoba_ref[...], wm_ref[...], preferred_element_type=F32)
    merged = g1_ref[...].astype(F32) * yr + g2_ref[...].astype(F32) * ym
    x1 = x_ref[...] + jnp.dot(merged.astype(BF16), wo_ref[...], preferred_element_type=F32)
    x1_ref[...] = x1
    h2 = x1 * lax.rsqrt(jnp.mean(x1 * x1, axis=-1, keepdims=True) + RMS_EPS) * gn_ref[...]
    h2_ref[...] = _pack_halves(h2)
    lg_ref[...] = _split_dot(h2, rw_ref[...], _NN) + rb_ref[...]


def _out_proj(o_ret, o_moba, proj, x2, wr, wm, wo, gn, rw_pad, rb_pad):
    t = x2.shape[0]
    gblk = OFF_GATE // D_MODEL
    row = lambda i: (i, 0)
    const = lambda i: (0, 0)
    wspec = pl.BlockSpec((D_MODEL, D_MODEL), const, pipeline_mode=pl.Buffered(1))
    return pl.pallas_call(
        _out_proj_kernel,
        grid=(t // _OUT_TM,),
        in_specs=[pl.BlockSpec((_OUT_TM, D_MODEL), row),
                  pl.BlockSpec((_OUT_TM, D_MODEL), row),
                  pl.BlockSpec((_OUT_TM, D_MODEL), lambda i: (i, gblk)),
                  pl.BlockSpec((_OUT_TM, D_MODEL), lambda i: (i, gblk + 1)),
                  pl.BlockSpec((_OUT_TM, D_MODEL), row),
                  wspec, wspec, wspec,
                  pl.BlockSpec((1, D_MODEL), const),
                  pl.BlockSpec((D_MODEL, LANES), const),
                  pl.BlockSpec((1, LANES), const)],
        out_specs=[pl.BlockSpec((_OUT_TM, D_MODEL), row),
                   pl.BlockSpec((_OUT_TM, D_MODEL // 2), row),
                   pl.BlockSpec((_OUT_TM, LANES), row)],
        out_shape=[jax.ShapeDtypeStruct((t, D_MODEL), F32),
                   jax.ShapeDtypeStruct((t, D_MODEL // 2), U32),
                   jax.ShapeDtypeStruct((t, LANES), F32)],
        compiler_params=_cparams(("arbitrary",), 56),
        name="out_proj",
    )(o_ret, o_moba, proj, proj, x2, wr, wm, wo, gn, rw_pad, rb_pad)


_RT_CHUNK = 256
_MAX_UNITS = 64


def _lane_scan(x, lane):
    s = 1
    while s < LANES:
        x = x + jnp.where(lane >= s, pltpu.roll(x, s, 1), 0.0)
        s *= 2
    return x


def _routing_kernel(lg_ref, dest_ref, comb_ref, unit_ref, idx_s, pos_s):
    t = lg_ref.shape[0]
    ch = _RT_CHUNK
    n_ch = t // ch
    lane_i = lax.broadcasted_iota(jnp.int32, (ch, LANES), 1)
    lane = lane_i.astype(F32)
    ri = lax.broadcasted_iota(jnp.int32, (ch, ch), 0)
    ci = lax.broadcasted_iota(jnp.int32, (ch, ch), 1)
    tri = (ci < ri).astype(BF16)

    def phase1(c, carry):
        sl = pl.ds(pl.multiple_of(c * ch, ch), ch)
        l = lg_ref[sl, :]
        onehot = jnp.zeros((ch, LANES), F32)
        vals, idxs = [], []
        for _ in range(TOP_K):
            m = jnp.max(l, axis=-1, keepdims=True)
            idx = jnp.min(jnp.where(l == m, lane, float(LANES)), axis=-1, keepdims=True)
            hit = lane == idx
            vals.append(m)
            idxs.append(idx)
            onehot = onehot + hit.astype(F32)
            l = jnp.where(hit, -jnp.inf, l)
        exps = [jnp.exp(v - vals[0]) for v in vals]
        denom = exps[0] + exps[1] + exps[2] + exps[3]
        before = jnp.dot(tri, onehot.astype(BF16), preferred_element_type=F32) + carry
        idx_row = jnp.zeros((ch, LANES), F32)
        pos_row = jnp.zeros((ch, LANES), F32)
        comb_row = jnp.zeros((ch, LANES), F32)
        for k in range(TOP_K):
            pos_k = jnp.sum(jnp.where(lane == idxs[k], before, 0.0), axis=-1, keepdims=True)
            idx_row = jnp.where(lane_i == k, idxs[k], idx_row)
            pos_row = jnp.where(lane_i == k, pos_k, pos_row)
            comb_row = jnp.where(lane_i == k, exps[k] / denom, comb_row)
        idx_s[sl, :] = idx_row
        pos_s[sl, :] = pos_row
        comb_ref[sl, :] = comb_row
        return carry + jnp.sum(onehot, axis=0, keepdims=True)

    counts = lax.fori_loop(0, n_ch, phase1, jnp.zeros((1, LANES), F32))
    lane8 = lax.broadcasted_iota(jnp.int32, (8, LANES), 1)
    counts8 = jnp.broadcast_to(counts, (8, LANES))
    nblk = jnp.floor((counts8 + (ROW_BLOCK - 1.0)) * (1.0 / ROW_BLOCK))
    padded = nblk * float(ROW_BLOCK)
    pstart = _lane_scan(padded, lane8) - padded
    pstart_row = pstart[0:1, :]

    def phase2(c, carry):
        sl = pl.ds(pl.multiple_of(c * ch, ch), ch)
        idx_row = idx_s[sl, :]
        pos_row = pos_s[sl, :]
        dest_row = jnp.zeros((ch, LANES), F32)
        for k in range(TOP_K):
            start_k = jnp.sum(jnp.where(lane == idx_row[:, k:k + 1], pstart_row, 0.0), axis=-1, keepdims=True)
            dest_row = jnp.where(lane_i == k, start_k + pos_row[:, k:k + 1], dest_row)
        dest_ref[sl, :] = dest_row.astype(jnp.int32)
        return carry

    lax.fori_loop(0, n_ch, phase2, 0)

    units_e = jnp.floor((nblk + (UNIT_BLOCKS - 0.5)) * (1.0 / UNIT_BLOCKS))
    ucum = _lane_scan(units_e, lane8)
    ucum_row = ucum[0:1, :]
    uexcl_row = ucum_row - units_e[0:1, :]
    nblk_row = nblk[0:1, :]
    pblk_row = pstart_row * (1.0 / ROW_BLOCK)
    lane_u = lax.broadcasted_iota(jnp.int32, (_MAX_UNITS, LANES), 1)
    lane_uf = lane_u.astype(F32)
    uu = lax.broadcasted_iota(jnp.int32, (_MAX_UNITS, LANES), 0).astype(F32)
    e_u = jnp.sum(jnp.where((ucum_row <= uu) & (lane_u < N_EXPERTS), 1.0, 0.0), axis=-1, keepdims=True)
    e_u = jnp.minimum(e_u, N_EXPERTS - 1.0)
    pick = lane_uf == e_u
    take = lambda row: jnp.sum(jnp.where(pick, row, 0.0), axis=-1, keepdims=True)
    k_in_e = uu[:, 0:1] - take(uexcl_row)
    nb_u = jnp.clip(take(nblk_row) - k_in_e * UNIT_BLOCKS, 0.0, float(UNIT_BLOCKS))
    rb0_u = take(pblk_row) + k_in_e * UNIT_BLOCKS
    total_blk = jnp.sum(jnp.where(lane_u < N_EXPERTS, nblk_row, 0.0), axis=-1, keepdims=True)
    table = jnp.where(lane_u == 0, e_u, jnp.where(lane_u == 1, rb0_u, jnp.where(lane_u == 2, nb_u,
                      jnp.where(lane_u == 3, total_blk, 0.0))))
    unit_ref[...] = table.astype(jnp.int32)


def _routing(logits):
    t = logits.shape[0]
    return pl.pallas_call(
        _routing_kernel,
        out_shape=[jax.ShapeDtypeStruct((t, LANES), jnp.int32),
                   jax.ShapeDtypeStruct((t, LANES), F32),
                   jax.ShapeDtypeStruct((_MAX_UNITS, LANES), jnp.int32)],
        scratch_shapes=[pltpu.VMEM((t, LANES), F32), pltpu.VMEM((t, LANES), F32)],
        compiler_params=pltpu.CompilerParams(vmem_limit_bytes=48 * MIB),
        name="routing",
    )(logits)


_DISP_TOK = 256


def _dispatch_kernel(dest_ref, h2_ref, zero_hbm, xr_hbm, sem):
    del zero_hbm

    def issue(tok, carry):
        for k in range(TOP_K):
            pltpu.make_async_copy(h2_ref.at[pl.ds(tok, 1)], xr_hbm.at[pl.ds(dest_ref[tok * TOP_K + k], 1)],
                                  sem).start(priority=k % 2)
        return carry

    lax.fori_loop(0, _DISP_TOK, issue, 0, unroll=4)
    for _ in range(TOP_K):
        pltpu.make_async_copy(h2_ref, xr_hbm.at[pl.ds(0, _DISP_TOK)], sem).wait()


def _dispatch(dest_flat, h2p, n_rows):
    t, w = h2p.shape
    zeros = jnp.zeros((n_rows, w), U32)
    return pl.pallas_call(
        _dispatch_kernel,
        grid=(t // _DISP_TOK,),
        in_specs=[pl.BlockSpec((_DISP_TOK * TOP_K,), lambda i: (i,), memory_space=pltpu.SMEM),
                  pl.BlockSpec((_DISP_TOK, w), lambda i: (i, 0)),
                  pl.BlockSpec(memory_space=pl.ANY)],
        out_specs=pl.BlockSpec(memory_space=pl.ANY),
        out_shape=jax.ShapeDtypeStruct((n_rows, w), U32),
        scratch_shapes=[pltpu.SemaphoreType.DMA(())],
        input_output_aliases={2: 0},
        compiler_params=_cparams(("arbitrary",), 16),
        name="dispatch",
    )(dest_flat, h2p, zeros)


_EXP_TF = 256
_EXP_J = D_FF // _EXP_TF
_EXP_TFB = 512
_EXP_JB = D_MODEL // _EXP_TFB
_EXP_STEPS = _EXP_J + _EXP_JB


def _blk_copy(src, s, dst, d, sem):
    return pltpu.make_async_copy(src.at[pl.ds(s, ROW_BLOCK)], dst.at[pl.ds(d, ROW_BLOCK)], sem)


def _experts_kernel(ue_ref, urb_ref, unb_ref, utot_ref, xr_hbm, wg_ref, wu_ref, bg_ref, bu_ref, wd_ref, bd_ref,
                    y_hbm, xf_ref, xb_ref, act_ref, out_ref, wg_bf, wu_bf, wd_bf, sem_in, sem_out):
    u = pl.program_id(0)
    s = pl.program_id(1)
    nb = unb_ref[u]
    half = D_MODEL // 2
    tf = _EXP_TF
    tfb = _EXP_TFB
    last_step = _EXP_STEPS - 1

    def load_x(unit, start):
        def body(r, c):
            cp = _blk_copy(xr_hbm, (urb_ref[unit] + r) * ROW_BLOCK, xf_ref, r * ROW_BLOCK, sem_in)
            cp.start() if start else cp.wait()
            return c

        lax.fori_loop(0, unb_ref[unit], body, 0)

    def store_y(unit, start):
        def body(r, c):
            cp = _blk_copy(out_ref, r * ROW_BLOCK, y_hbm, (urb_ref[unit] + r) * ROW_BLOCK, sem_out)
            cp.start() if start else cp.wait()
            return c

        lax.fori_loop(0, unb_ref[unit], body, 0)

    prev = jnp.maximum(u - 1, 0)
    nxt = jnp.minimum(u + 1, _MAX_UNITS - 1)
    prev_pending = (u > 0) & (unb_ref[prev] > 0)

    @pl.when((u == 0) & (s == 0))
    def _():
        xb_ref[...] = jnp.zeros_like(xb_ref)
        load_x(u, True)

    @pl.when(prev_pending & (((nb > 0) & (s == _EXP_J)) | ((nb == 0) & (s == 0))))
    def _():
        store_y(prev, False)

    @pl.when(nb > 0)
    def _():
        @pl.when(s == 0)
        def _():
            load_x(u, False)

            def cast(r, c):
                sl = pl.ds(pl.multiple_of(r * ROW_BLOCK, ROW_BLOCK), ROW_BLOCK)
                hi, lo = _unpack_halves(xf_ref[sl, :])
                xb_ref[sl, :half] = hi
                xb_ref[sl, half:] = lo
                return c

            lax.fori_loop(0, nb, cast, 0)

        def activation(rows):
            x = xb_ref[0:rows, :]
            g = jnp.dot(x, wg_bf[...], preferred_element_type=F32) + bg_ref[...]
            up = jnp.dot(x, wu_bf[...], preferred_element_type=F32) + bu_ref[...]
            gate = jnp.minimum(g, SWIGLU_LIMIT)
            up = jnp.clip(up, -SWIGLU_LIMIT, SWIGLU_LIMIT)
            act = (up + 1.0) * gate * jax.nn.sigmoid(SWIGLU_ALPHA * gate)
            act_ref[0:rows, pl.ds(pl.multiple_of(s * tf, tf), tf)] = act.astype(BF16)

        def down(rows):
            y = jnp.dot(act_ref[0:rows, :], wd_bf[...], preferred_element_type=F32) + bd_ref[...]
            out_ref[0:rows, pl.ds(pl.multiple_of((s - _EXP_J) * tfb, tfb), tfb)] = y

        short = nb < UNIT_BLOCKS

        @pl.when(s < _EXP_J)
        def _():
            wg_bf[...] = wg_ref[...].astype(BF16)
            wu_bf[...] = wu_ref[...].astype(BF16)
            pl.when(short)(lambda: activation(UNIT_ROWS - ROW_BLOCK))
            pl.when(jnp.logical_not(short))(lambda: activation(UNIT_ROWS))

        @pl.when((s == _EXP_J) & (u + 1 < _MAX_UNITS))
        def _():
            load_x(nxt, True)

        @pl.when(s >= _EXP_J)
        def _():
            wd_bf[...] = wd_ref[...].astype(BF16)
            pl.when(short)(lambda: down(UNIT_ROWS - ROW_BLOCK))
            pl.when(jnp.logical_not(short))(lambda: down(UNIT_ROWS))

        @pl.when(s == last_step)
        def _():
            store_y(u, True)

    @pl.when((u == _MAX_UNITS - 1) & (s == last_step))
    def _():
        @pl.when(nb > 0)
        def _():
            store_y(u, False)

        out_ref[0:ROW_BLOCK, :] = jnp.zeros((ROW_BLOCK, D_MODEL), F32)
        first = utot_ref[0]
        n_blocks = y_hbm.shape[0] // ROW_BLOCK

        def start(r, c):
            _blk_copy(out_ref, 0, y_hbm, r * ROW_BLOCK, sem_out).start()
            return c

        lax.fori_loop(first, n_blocks, start, 0)

        def finish(r, c):
            _blk_copy(out_ref, 0, y_hbm, 0, sem_out).wait()
            return c

        lax.fori_loop(first, n_blocks, finish, 0)


def _experts(unit_e, unit_rb, unit_nb, unit_tot, x_rows, w_gu, b_gu3, w_down, b_down3):
    n_rows = x_rows.shape[0]
    jlast = _EXP_J - 1

    def ja(u, s, nb):
        return jnp.where(nb[u] > 0, jnp.minimum(s, jlast), jlast)

    def jb(u, s, nb):
        return jnp.where(nb[u] > 0, jnp.maximum(s - _EXP_J, 0), _EXP_JB - 1)

    tf = _EXP_TF
    tfb = _EXP_TFB
    return pl.pallas_call(
        _experts_kernel,
        grid_spec=pltpu.PrefetchScalarGridSpec(
            num_scalar_prefetch=4,
            grid=(_MAX_UNITS, _EXP_STEPS),
            in_specs=[pl.BlockSpec(memory_space=pl.ANY),
                      pl.BlockSpec((None, D_MODEL, tf), lambda u, s, e, rb, nb, tot: (e[u], 0, ja(u, s, nb))),
                      pl.BlockSpec((None, D_MODEL, tf),
                                   lambda u, s, e, rb, nb, tot: (e[u], 0, _EXP_J + ja(u, s, nb))),
                      pl.BlockSpec((None, 1, tf), lambda u, s, e, rb, nb, tot: (e[u], 0, ja(u, s, nb))),
                      pl.BlockSpec((None, 1, tf), lambda u, s, e, rb, nb, tot: (e[u], 0, _EXP_J + ja(u, s, nb))),
                      pl.BlockSpec((None, D_FF, tfb), lambda u, s, e, rb, nb, tot: (e[u], 0, jb(u, s, nb))),
                      pl.BlockSpec((None, 1, tfb), lambda u, s, e, rb, nb, tot: (e[u], 0, jb(u, s, nb)))],
            out_specs=pl.BlockSpec(memory_space=pl.ANY),
            scratch_shapes=[pltpu.VMEM((UNIT_ROWS, D_MODEL // 2), U32),
                            pltpu.VMEM((UNIT_ROWS, D_MODEL), BF16),
                            pltpu.VMEM((UNIT_ROWS, D_FF), BF16),
                            pltpu.VMEM((UNIT_ROWS, D_MODEL), F32),
                            pltpu.VMEM((D_MODEL, tf), BF16),
                            pltpu.VMEM((D_MODEL, tf), BF16),
                            pltpu.VMEM((D_FF, tfb), BF16),
                            pltpu.SemaphoreType.DMA(()),
                            pltpu.SemaphoreType.DMA(())]),
        out_shape=jax.ShapeDtypeStruct((n_rows, D_MODEL), F32),
        compiler_params=_cparams(("arbitrary", "arbitrary"), 58),
        name="experts",
    )(unit_e, unit_rb, unit_nb, unit_tot, x_rows, w_gu, w_gu, b_gu3, b_gu3, w_down, b_down3)


_CMB_TOK = 256


def _combine_kernel(dest_ref, dest_next_ref, x1_ref, comb_ref, gf_ref, y_hbm, o_ref, ybuf, sem):
    i = pl.program_id(0)
    n = pl.num_programs(0)
    slot = i % 2

    def gather(d_ref, into):
        def issue(tok, carry):
            for k in range(TOP_K):
                pltpu.make_async_copy(y_hbm.at[pl.ds(d_ref[tok * TOP_K + k], 1)],
                                      ybuf.at[into, k, pl.ds(tok, 1)], sem.at[into]).start(priority=k % 2)
            return carry

        lax.fori_loop(0, _CMB_TOK, issue, 0, unroll=4)

    @pl.when(i == 0)
    def _():
        gather(dest_ref, 0)

    @pl.when(i + 1 < n)
    def _():
        gather(dest_next_ref, 1 - slot)

    for k in range(TOP_K):
        pltpu.make_async_copy(y_hbm.at[pl.ds(0, _CMB_TOK)], ybuf.at[slot, k], sem.at[slot]).wait()

    comb = comb_ref[...]
    x2 = x1_ref[...]
    for k in range(TOP_K):
        x2 = x2 + comb[:, k:k + 1] * ybuf[slot, k]
    o_ref[...] = x2 * lax.rsqrt(jnp.mean(x2 * x2, axis=-1, keepdims=True) + RMS_EPS) * gf_ref[...]


def _combine(dest_flat, x1, comb, g_final, y_rows):
    t = x1.shape[0]
    n = t // _CMB_TOK
    row = lambda i: (i, 0)
    return pl.pallas_call(
        _combine_kernel,
        grid=(n,),
        in_specs=[pl.BlockSpec((_CMB_TOK * TOP_K,), lambda i: (i,), memory_space=pltpu.SMEM),
                  pl.BlockSpec((_CMB_TOK * TOP_K,), lambda i: (jnp.minimum(i + 1, n - 1),),
                               memory_space=pltpu.SMEM),
                  pl.BlockSpec((_CMB_TOK, D_MODEL), row),
                  pl.BlockSpec((_CMB_TOK, LANES), row),
                  pl.BlockSpec((1, D_MODEL), lambda i: (0, 0)),
                  pl.BlockSpec(memory_space=pl.ANY)],
        out_specs=pl.BlockSpec((_CMB_TOK, D_MODEL), row),
        out_shape=jax.ShapeDtypeStruct((t, D_MODEL), F32),
        scratch_shapes=[pltpu.VMEM((2, TOP_K, _CMB_TOK, D_MODEL), F32), pltpu.SemaphoreType.DMA((2,))],
        compiler_params=_cparams(("arbitrary",), 40),
        name="combine",
    )(dest_flat, dest_flat, x1, comb, g_final, y_rows)


def kernel(x, positions, norm_mix_g, w_in, ret_w_o, moba_w_o, w_out, norm_ffn_g, router_w, router_b,
           exp_w_gu, exp_b_gu, exp_w_down, exp_b_down, norm_final_g):
    batch, seq, d = x.shape
    t = batch * seq
    depth = w_in.shape[0]
    assert depth == 1, "the combine kernel applies the closing norm, so exactly one layer is supported"
    half = MOBA_HEAD_DIM // 2
    inv_freq = ROPE_THETA ** (-jnp.arange(half, dtype=F32) / half)
    inv_full = jnp.concatenate([inv_freq, inv_freq])[None, :]
    log_gamma = jnp.log1p(-jnp.exp2(-5.0 - jnp.arange(RET_HEADS, dtype=F32)))
    cos_t, sin_t = _rope_tables(positions.reshape(t, 1), inv_full)
    n_rows = (-(-(t * TOP_K) // ROW_BLOCK) + N_EXPERTS) * ROW_BLOCK

    x2 = x.reshape(t, d)
    for l in range(depth):
        proj = _in_proj(x2, norm_mix_g[l][None, :], w_in[l].astype(BF16), cos_t, sin_t)
        o_ret = _retention(log_gamma, proj, batch, seq)
        o_moba = _moba(proj, batch, seq)
        rw_pad = jnp.pad(router_w[l], ((0, 0), (0, LANES - N_EXPERTS)))
        rb_pad = jnp.pad(router_b[l], (0, LANES - N_EXPERTS), constant_values=NEG_INF)[None, :]
        x1, h2, logits = _out_proj(o_ret, o_moba, proj, x2, ret_w_o[l].astype(BF16), moba_w_o[l].astype(BF16),
                                   w_out[l].astype(BF16), norm_ffn_g[l][None, :], rw_pad, rb_pad)
        dest, comb, units = _routing(logits)
        dest_flat = dest[:, :TOP_K].reshape(t * TOP_K)
        x_rows = _dispatch(dest_flat, h2, n_rows)
        y_rows = _experts(units[:, 0], units[:, 1], units[:, 2], units[0:1, 3], x_rows, exp_w_gu[l],
                          exp_b_gu[l][:, None, :], exp_w_down[l], exp_b_down[l][:, None, :])
        x2 = _combine(dest_flat, x1, comb, norm_final_g[None, :], y_rows)
    return x2.reshape(batch, seq, d)
```

```python
import functools

import jax
import jax.numpy as jnp
from jax import lax
from jax.experimental import pallas as pl
from jax.experimental.pallas import tpu as pltpu

F32 = jnp.float32
BF16 = jnp.bfloat16
U32 = jnp.uint32

D_MODEL = 2048
RET_HEADS = 8
RET_QK_DIM = 128
RET_V_DIM = 256
RET_CHUNK = 128
MOBA_HEADS = 16
MOBA_HEAD_DIM = 128
MOBA_BLOCK = 256
MOBA_TOPK = 3
ROPE_THETA = 10000.0
NEG_INF = -1e30
N_EXPERTS = 32
TOP_K = 4
D_FF = D_MODEL
SWIGLU_LIMIT = 7.0
SWIGLU_ALPHA = 1.702
RMS_EPS = 1e-5

RET_QK_W = RET_HEADS * RET_QK_DIM
RET_V_W = RET_HEADS * RET_V_DIM
MOBA_W = MOBA_HEADS * MOBA_HEAD_DIM
IN_WIDTH = 2 * RET_QK_W + 2 * RET_V_W + 3 * MOBA_W + 2 * D_MODEL
OFF_RQ = 0
OFF_RK = OFF_RQ + RET_QK_W
OFF_RV = OFF_RK + RET_QK_W
OFF_RG = OFF_RV + RET_V_W
OFF_MQ = OFF_RG + RET_V_W
OFF_MK = OFF_MQ + MOBA_W
OFF_MV = OFF_MK + MOBA_W
OFF_GATE = OFF_MV + MOBA_W

LANES = 128
ROW_BLOCK = 256
UNIT_BLOCKS = 5
UNIT_ROWS = UNIT_BLOCKS * ROW_BLOCK
MIB = 1024 * 1024


def _cparams(sem, vmem_mib):
    return pltpu.CompilerParams(dimension_semantics=sem, vmem_limit_bytes=vmem_mib * MIB)


def _split_dot(a, b, dims):
    ah = a.astype(BF16)
    al = (a - ah.astype(F32)).astype(BF16)
    bh = b.astype(BF16)
    bl = (b - bh.astype(F32)).astype(BF16)
    d = functools.partial(lax.dot_general, dimension_numbers=dims, preferred_element_type=F32)
    return d(ah, bh) + (d(ah, bl) + d(al, bh))


def _pack_halves(x):
    n = x.shape[1] // 2
    bits = lax.bitcast_convert_type(x.astype(BF16).astype(F32), U32)
    return bits[:, :n] | lax.shift_right_logical(bits[:, n:], jnp.uint32(16))


def _unpack_halves(u):
    hi = lax.bitcast_convert_type(u & jnp.uint32(0xFFFF0000), F32).astype(BF16)
    lo = lax.bitcast_convert_type(lax.shift_left(u, jnp.uint32(16)), F32).astype(BF16)
    return hi, lo


_NN = (((1,), (0,)), ((), ()))
_NT = (((1,), (1,)), ((), ()))


def _rope_kernel(pos_ref, inv_ref, cos_ref, sin_ref):
    ang = pos_ref[...].astype(F32) * inv_ref[...]
    lane = lax.broadcasted_iota(jnp.int32, ang.shape, 1)
    cos_ref[...] = jnp.cos(ang)
    sin_ref[...] = jnp.where(lane < LANES // 2, -jnp.sin(ang), jnp.sin(ang))


def _rope_tables(pos_col, inv_full):
    t = pos_col.shape[0]
    tm = 1024
    return pl.pallas_call(
        _rope_kernel,
        grid=(t // tm,),
        in_specs=[pl.BlockSpec((tm, 1), lambda i: (i, 0)),
                  pl.BlockSpec((1, LANES), lambda i: (0, 0))],
        out_specs=[pl.BlockSpec((tm, LANES), lambda i: (i, 0))] * 2,
        out_shape=[jax.ShapeDtypeStruct((t, LANES), F32)] * 2,
        compiler_params=_cparams(("arbitrary",), 16),
        name="rope_tables",
    )(pos_col, inv_full)


_IN_TM = 1024
_IN_TN = 1024


def _in_proj_kernel(x_ref, g_ref, w_ref, cos_ref, sin_ref, o_ref, h_ref):
    j = pl.program_id(1)

    @pl.when(j == 0)
    def _():
        x = x_ref[...]
        ms = jnp.mean(x * x, axis=-1, keepdims=True)
        h_ref[...] = (x * lax.rsqrt(ms + RMS_EPS) * g_ref[...]).astype(BF16)

    acc = jnp.dot(h_ref[...], w_ref[...], preferred_element_type=F32)
    col = j * _IN_TN
    is_rk = (col >= OFF_RK) & (col < OFF_RV)
    is_rot = (col < OFF_RV) | ((col >= OFF_MQ) & (col < OFF_MV))
    is_sig = col >= OFF_GATE

    @pl.when(is_rot)
    def _():
        scale = jnp.where(is_rk, RET_QK_DIM ** -0.5, 1.0).astype(F32)
        cos = cos_ref[...]
        sin = sin_ref[...]
        for hh in range(_IN_TN // LANES):
            blk = acc[:, hh * LANES:(hh + 1) * LANES]
            rot = blk * cos + pltpu.roll(blk, LANES // 2, 1) * sin
            o_ref[:, hh * LANES:(hh + 1) * LANES] = (rot * scale).astype(BF16)

    @pl.when(is_sig)
    def _():
        o_ref[...] = jax.nn.sigmoid(acc).astype(BF16)

    @pl.when(jnp.logical_not(is_rot | is_sig))
    def _():
        o_ref[...] = acc.astype(BF16)


def _in_proj(x2, g, w_bf, cos_t, sin_t):
    t = x2.shape[0]
    return pl.pallas_call(
        _in_proj_kernel,
        grid=(t // _IN_TM, IN_WIDTH // _IN_TN),
        in_specs=[pl.BlockSpec((_IN_TM, D_MODEL), lambda i, j: (i, 0)),
                  pl.BlockSpec((1, D_MODEL), lambda i, j: (0, 0)),
                  pl.BlockSpec((D_MODEL, _IN_TN), lambda i, j: (0, j)),
                  pl.BlockSpec((_IN_TM, LANES), lambda i, j: (i, 0)),
                  pl.BlockSpec((_IN_TM, LANES), lambda i, j: (i, 0))],
        out_specs=pl.BlockSpec((_IN_TM, _IN_TN), lambda i, j: (i, j)),
        out_shape=jax.ShapeDtypeStruct((t, IN_WIDTH), BF16),
        scratch_shapes=[pltpu.VMEM((_IN_TM, D_MODEL), BF16)],
        compiler_params=_cparams(("arbitrary", "arbitrary"), 56),
        name="in_proj",
    )(x2, g, w_bf, cos_t, sin_t)


def _retention_kernel(lg_ref, q_ref, k_ref, v_ref, g_ref, o_ref, state_ref):
    c = RET_CHUNK
    lg = lg_ref[pl.program_id(1)]
    ii = lax.broadcasted_iota(jnp.int32, (c, c), 0)
    jj = lax.broadcasted_iota(jnp.int32, (c, c), 1)
    diff = (ii - jj).astype(F32)
    decay = jnp.where(diff >= 0.0, jnp.exp(jnp.maximum(diff, 0.0) * lg), 0.0)
    pos = lax.broadcasted_iota(jnp.int32, (c, 1), 0).astype(F32)
    k_w = jnp.exp((c - 1.0 - pos) * lg)
    q_w = jnp.exp((pos + 1.0) * lg)
    chunk_decay = jnp.exp(jnp.full((1, RET_V_DIM), float(c), F32) * lg)
    state_ref[...] = jnp.zeros_like(state_ref)

    def body(n, carry):
        sl = pl.ds(pl.multiple_of(n * c, c), c)
        q = q_ref[sl, :]
        k = k_ref[sl, :]
        v = v_ref[sl, :]
        s = lax.dot_general(q, k, _NT, preferred_element_type=F32) * decay
        o = jnp.dot(s.astype(BF16), v, preferred_element_type=F32)
        state = state_ref[...]
        qs = (q.astype(F32) * q_w).astype(BF16)
        o = o + jnp.dot(qs, state.astype(BF16), preferred_element_type=F32)
        kt = (k.astype(F32) * k_w).T.astype(BF16)
        state_ref[...] = chunk_decay * state + jnp.dot(kt, v, preferred_element_type=F32)
        on = o * lax.rsqrt(jnp.mean(o * o, axis=-1, keepdims=True) + RMS_EPS)
        gg = g_ref[sl, :].astype(F32)
        o_ref[sl, :] = (on * (gg * jax.nn.sigmoid(gg))).astype(BF16)
        return carry

    lax.fori_loop(0, q_ref.shape[0] // c, body, 0, unroll=8)


def _retention(log_gamma, proj, batch, seq):
    qb, vb = OFF_RQ // RET_QK_DIM, OFF_RV // RET_V_DIM
    kb, gb = OFF_RK // RET_QK_DIM, OFF_RG // RET_V_DIM
    return pl.pallas_call(
        _retention_kernel,
        grid_spec=pltpu.PrefetchScalarGridSpec(
            num_scalar_prefetch=1,
            grid=(batch, RET_HEADS),
            in_specs=[pl.BlockSpec((seq, RET_QK_DIM), lambda b, h, lg: (b, qb + h)),
                      pl.BlockSpec((seq, RET_QK_DIM), lambda b, h, lg: (b, kb + h)),
                      pl.BlockSpec((seq, RET_V_DIM), lambda b, h, lg: (b, vb + h)),
                      pl.BlockSpec((seq, RET_V_DIM), lambda b, h, lg: (b, gb + h))],
            out_specs=pl.BlockSpec((seq, RET_V_DIM), lambda b, h, lg: (b, h)),
            scratch_shapes=[pltpu.VMEM((RET_QK_DIM, RET_V_DIM), F32)]),
        out_shape=jax.ShapeDtypeStruct((batch * seq, RET_V_W), BF16),
        compiler_params=_cparams(("arbitrary", "arbitrary"), 32),
        name="retention",
    )(log_gamma, proj, proj, proj, proj)


def _moba_kernel(q_ref, k_ref, v_ref, o_ref):
    L = MOBA_BLOCK
    seq = q_ref.shape[0]
    n_kb = seq // L
    scale = MOBA_HEAD_DIM ** -0.5
    kmean = jnp.concatenate(
        [jnp.mean(k_ref[j * L:(j + 1) * L, :].astype(F32), axis=0, keepdims=True) for j in range(n_kb)]
        + [jnp.zeros((LANES - n_kb, MOBA_HEAD_DIM), F32)], axis=0)
    qi_idx = lax.broadcasted_iota(jnp.int32, (L, L), 0)
    kj_idx = lax.broadcasted_iota(jnp.int32, (L, L), 1)
    causal = kj_idx <= qi_idx
    for i in range(n_kb):
        q = q_ref[i * L:(i + 1) * L, :]
        sel = None
        if i > MOBA_TOPK:
            gate = _split_dot(q.astype(F32), kmean, _NT)
            cols = [gate[:, j:j + 1] for j in range(i)]
            sel = []
            for j in range(i):
                rank = jnp.zeros((L, 1), F32)
                for j2 in range(i):
                    if j2 == j:
                        continue
                    ahead = (cols[j2] > cols[j]) | ((cols[j2] == cols[j]) & (j2 < j))
                    rank = rank + ahead.astype(F32)
                sel.append(rank < float(MOBA_TOPK))
        s_blocks = []
        m = None
        for j in range(i + 1):
            s = lax.dot_general(q, k_ref[j * L:(j + 1) * L, :], _NT, preferred_element_type=F32) * scale
            if j == i:
                s = jnp.where(causal, s, NEG_INF)
            elif sel is not None:
                s = jnp.where(sel[j], s, NEG_INF)
            s_blocks.append(s)
            mj = jnp.max(s, axis=-1, keepdims=True)
            m = mj if m is None else jnp.maximum(m, mj)
        acc = jnp.zeros((L, MOBA_HEAD_DIM), F32)
        denom = jnp.zeros((L, 1), F32)
        for j in range(i + 1):
            p = jnp.exp(s_blocks[j] - m)
            denom = denom + jnp.sum(p, axis=-1, keepdims=True)
            acc = acc + jnp.dot(p.astype(BF16), v_ref[j * L:(j + 1) * L, :], preferred_element_type=F32)
        o_ref[i * L:(i + 1) * L, :] = (acc / denom).astype(BF16)


def _moba(proj, batch, seq):
    qb, kb, vb = OFF_MQ // MOBA_HEAD_DIM, OFF_MK // MOBA_HEAD_DIM, OFF_MV // MOBA_HEAD_DIM
    return pl.pallas_call(
        _moba_kernel,
        grid=(batch, MOBA_HEADS),
        in_specs=[pl.BlockSpec((seq, MOBA_HEAD_DIM), lambda b, h: (b, qb + h)),
                  pl.BlockSpec((seq, MOBA_HEAD_DIM), lambda b, h: (b, kb + h)),
                  pl.BlockSpec((seq, MOBA_HEAD_DIM), lambda b, h: (b, vb + h))],
        out_specs=pl.BlockSpec((seq, MOBA_HEAD_DIM), lambda b, h: (b, h)),
        out_shape=jax.ShapeDtypeStruct((batch * seq, MOBA_W), BF16),
        compiler_params=_cparams(("arbitrary", "arbitrary"), 32),
        name="moba",
    )(proj, proj, proj)


_OUT_TM = 256


def _out_proj_kernel(oret_ref, omoba_ref, g1_ref, g2_ref, x_ref, wr_ref, wm_ref, wo_ref,
                     gn_ref, rw_ref, rb_ref, x1_ref, h2_ref, lg_ref):
    yr = jnp.dot(oret_ref[...], wr_ref[...], preferred_element_type=F32)
    ym = jnp.dot(omoba_ref[...], wm_ref[...], preferred_element_type=F32)
    merged = g1_ref[...].astype(F32) * yr + g2_ref[...].astype(F32) * ym
    x1 = x_ref[...] + jnp.dot(merged.astype(BF16), wo_ref[...], preferred_element_type=F32)
    x1_ref[...] = x1
    h2 = x1 * lax.rsqrt(jnp.mean(x1 * x1, axis=-1, keepdims=True) + RMS_EPS) * gn_ref[...]
    h2_ref[...] = _pack_halves(h2)
    lg_ref[...] = _split_dot(h2, rw_ref[...], _NN) + rb_ref[...]


def _out_proj(o_ret, o_moba, proj, x2, wr, wm, wo, gn, rw_pad, rb_pad):
    t = x2.shape[0]
    gblk = OFF_GATE // D_MODEL
    row = lambda i: (i, 0)
    const = lambda i: (0, 0)
    wspec = pl.BlockSpec((D_MODEL, D_MODEL), const, pipeline_mode=pl.Buffered(1))
    return pl.pallas_call(
        _out_proj_kernel,
        grid=(t // _OUT_TM,),
        in_specs=[pl.BlockSpec((_OUT_TM, D_MODEL), row),
                  pl.BlockSpec((_OUT_TM, D_MODEL), row),
                  pl.BlockSpec((_OUT_TM, D_MODEL), lambda i: (i, gblk)),
                  pl.BlockSpec((_OUT_TM, D_MODEL), lambda i: (i, gblk + 1)),
                  pl.BlockSpec((_OUT_TM, D_MODEL), row),
                  wspec, wspec, wspec,
                  pl.BlockSpec((1, D_MODEL), const),
                  pl.BlockSpec((D_MODEL, LANES), const),
                  pl.BlockSpec((1, LANES), const)],
        out_specs=[pl.BlockSpec((_OUT_TM, D_MODEL), row),
                   pl.BlockSpec((_OUT_TM, D_MODEL // 2), row),
                   pl.BlockSpec((_OUT_TM, LANES), row)],
        out_shape=[jax.ShapeDtypeStruct((t, D_MODEL), F32),
                   jax.ShapeDtypeStruct((t, D_MODEL // 2), U32),
                   jax.ShapeDtypeStruct((t, LANES), F32)],
        compiler_params=_cparams(("arbitrary",), 56),
        name="out_proj",
    )(o_ret, o_moba, proj, proj, x2, wr, wm, wo, gn, rw_pad, rb_pad)


_RT_CHUNK = 256
_MAX_UNITS = 64


def _lane_scan(x, lane):
    s = 1
    while s < LANES:
        x = x + jnp.where(lane >= s, pltpu.roll(x, s, 1), 0.0)
        s *= 2
    return x


def _routing_kernel(lg_ref, dest_ref, comb_ref, unit_ref, idx_s, pos_s):
    t = lg_ref.shape[0]
    ch = _RT_CHUNK
    n_ch = t // ch
    lane_i = lax.broadcasted_iota(jnp.int32, (ch, LANES), 1)
    lane = lane_i.astype(F32)
    ri = lax.broadcasted_iota(jnp.int32, (ch, ch), 0)
    ci = lax.broadcasted_iota(jnp.int32, (ch, ch), 1)
    tri = (ci < ri).astype(BF16)

    def phase1(c, carry):
        sl = pl.ds(pl.multiple_of(c * ch, ch), ch)
        l = lg_ref[sl, :]
        onehot = jnp.zeros((ch, LANES), F32)
        vals, idxs = [], []
        for _ in range(TOP_K):
            m = jnp.max(l, axis=-1, keepdims=True)
            idx = jnp.min(jnp.where(l == m, lane, float(LANES)), axis=-1, keepdims=True)
            hit = lane == idx
            vals.append(m)
            idxs.append(idx)
            onehot = onehot + hit.astype(F32)
            l = jnp.where(hit, -jnp.inf, l)
        exps = [jnp.exp(v - vals[0]) for v in vals]
        denom = exps[0] + exps[1] + exps[2] + exps[3]
        before = jnp.dot(tri, onehot.astype(BF16), preferred_element_type=F32) + carry
        idx_row = jnp.zeros((ch, LANES), F32)
        pos_row = jnp.zeros((ch, LANES), F32)
        comb_row = jnp.zeros((ch, LANES), F32)
        for k in range(TOP_K):
            pos_k = jnp.sum(jnp.where(lane == idxs[k], before, 0.0), axis=-1, keepdims=True)
            idx_row = jnp.where(lane_i == k, idxs[k], idx_row)
            pos_row = jnp.where(lane_i == k, pos_k, pos_row)
            comb_row = jnp.where(lane_i == k, exps[k] / denom, comb_row)
        idx_s[sl, :] = idx_row
        pos_s[sl, :] = pos_row
        comb_ref[sl, :] = comb_row
        return carry + jnp.sum(onehot, axis=0, keepdims=True)

    counts = lax.fori_loop(0, n_ch, phase1, jnp.zeros((1, LANES), F32))
    lane8 = lax.broadcasted_iota(jnp.int32, (8, LANES), 1)
    counts8 = jnp.broadcast_to(counts, (8, LANES))
    nblk = jnp.floor((counts8 + (ROW_BLOCK - 1.0)) * (1.0 / ROW_BLOCK))
    padded = nblk * float(ROW_BLOCK)
    pstart = _lane_scan(padded, lane8) - padded
    pstart_row = pstart[0:1, :]

    def phase2(c, carry):
        sl = pl.ds(pl.multiple_of(c * ch, ch), ch)
        idx_row = idx_s[sl, :]
        pos_row = pos_s[sl, :]
        dest_row = jnp.zeros((ch, LANES), F32)
        for k in range(TOP_K):
            start_k = jnp.sum(jnp.where(lane == idx_row[:, k:k + 1], pstart_row, 0.0), axis=-1, keepdims=True)
            dest_row = jnp.where(lane_i == k, start_k + pos_row[:, k:k + 1], dest_row)
        dest_ref[sl, :] = dest_row.astype(jnp.int32)
        return carry

    lax.fori_loop(0, n_ch, phase2, 0)

    units_e = jnp.floor((nblk + (UNIT_BLOCKS - 0.5)) * (1.0 / UNIT_BLOCKS))
    ucum = _lane_scan(units_e, lane8)
    ucum_row = ucum[0:1, :]
    uexcl_row = ucum_row - units_e[0:1, :]
    nblk_row = nblk[0:1, :]
    pblk_row = pstart_row * (1.0 / ROW_BLOCK)
    lane_u = lax.broadcasted_iota(jnp.int32, (_MAX_UNITS, LANES), 1)
    lane_uf = lane_u.astype(F32)
    uu = lax.broadcasted_iota(jnp.int32, (_MAX_UNITS, LANES), 0).astype(F32)
    e_u = jnp.sum(jnp.where((ucum_row <= uu) & (lane_u < N_EXPERTS), 1.0, 0.0), axis=-1, keepdims=True)
    e_u = jnp.minimum(e_u, N_EXPERTS - 1.0)
    pick = lane_uf == e_u
    take = lambda row: jnp.sum(jnp.where(pick, row, 0.0), axis=-1, keepdims=True)
    k_in_e = uu[:, 0:1] - take(uexcl_row)
    nb_u = jnp.clip(take(nblk_row) - k_in_e * UNIT_BLOCKS, 0.0, float(UNIT_BLOCKS))
    rb0_u = take(pblk_row) + k_in_e * UNIT_BLOCKS
    total_blk = jnp.sum(jnp.where(lane_u < N_EXPERTS, nblk_row, 0.0), axis=-1, keepdims=True)
    table = jnp.where(lane_u == 0, e_u, jnp.where(lane_u == 1, rb0_u, jnp.where(lane_u == 2, nb_u,
                      jnp.where(lane_u == 3, total_blk, 0.0))))
    unit_ref[...] = table.astype(jnp.int32)


def _routing(logits):
    t = logits.shape[0]
    return pl.pallas_call(
        _routing_kernel,
        out_shape=[jax.ShapeDtypeStruct((t, LANES), jnp.int32),
                   jax.ShapeDtypeStruct((t, LANES), F32),
                   jax.ShapeDtypeStruct((_MAX_UNITS, LANES), jnp.int32)],
        scratch_shapes=[pltpu.VMEM((t, LANES), F32), pltpu.VMEM((t, LANES), F32)],
        compiler_params=pltpu.CompilerParams(vmem_limit_bytes=48 * MIB),
        name="routing",
    )(logits)


_DISP_TOK = 256


def _dispatch_kernel(dest_ref, h2_ref, zero_hbm, xr_hbm, sem):
    del zero_hbm

    def issue(tok, carry):
        for k in range(TOP_K):
            pltpu.make_async_copy(h2_ref.at[pl.ds(tok, 1)], xr_hbm.at[pl.ds(dest_ref[tok * TOP_K + k], 1)],
                                  sem).start(priority=k % 2)
        return carry

    lax.fori_loop(0, _DISP_TOK, issue, 0, unroll=4)
    for _ in range(TOP_K):
        pltpu.make_async_copy(h2_ref, xr_hbm.at[pl.ds(0, _DISP_TOK)], sem).wait()


def _dispatch(dest_flat, h2p, n_rows):
    t, w = h2p.shape
    zeros = jnp.zeros((n_rows, w), U32)
    return pl.pallas_call(
        _dispatch_kernel,
        grid=(t // _DISP_TOK,),
        in_specs=[pl.BlockSpec((_DISP_TOK * TOP_K,), lambda i: (i,), memory_space=pltpu.SMEM),
                  pl.BlockSpec((_DISP_TOK, w), lambda i: (i, 0)),
                  pl.BlockSpec(memory_space=pl.ANY)],
        out_specs=pl.BlockSpec(memory_space=pl.ANY),
        out_shape=jax.ShapeDtypeStruct((n_rows, w), U32),
        scratch_shapes=[pltpu.SemaphoreType.DMA(())],
        input_output_aliases={2: 0},
        compiler_params=_cparams(("arbitrary",), 16),
        name="dispatch",
    )(dest_flat, h2p, zeros)


_EXP_TF = 256
_EXP_J = D_FF // _EXP_TF
_EXP_TFB = 512
_EXP_JB = D_MODEL // _EXP_TFB
_EXP_STEPS = _EXP_J + _EXP_JB


def _blk_copy(src, s, dst, d, sem):
    return pltpu.make_async_copy(src.at[pl.ds(s, ROW_BLOCK)], dst.at[pl.ds(d, ROW_BLOCK)], sem)


def _experts_kernel(ue_ref, urb_ref, unb_ref, utot_ref, xr_hbm, wg_ref, wu_ref, bg_ref, bu_ref, wd_ref, bd_ref,
                    y_hbm, xf_ref, xb_ref, act_ref, out_ref, wg_bf, wu_bf, wd_bf, sem_in, sem_out):
    u = pl.program_id(0)
    s = pl.program_id(1)
    nb = unb_ref[u]
    half = D_MODEL // 2
    tf = _EXP_TF
    tfb = _EXP_TFB
    last_step = _EXP_STEPS - 1

    def load_x(unit, start):
        def body(r, c):
            cp = _blk_copy(xr_hbm, (urb_ref[unit] + r) * ROW_BLOCK, xf_ref, r * ROW_BLOCK, sem_in)
            cp.start() if start else cp.wait()
            return c

        lax.fori_loop(0, unb_ref[unit], body, 0)

    def store_y(unit, start):
        def body(r, c):
            cp = _blk_copy(out_ref, r * ROW_BLOCK, y_hbm, (urb_ref[unit] + r) * ROW_BLOCK, sem_out)
            cp.start() if start else cp.wait()
            return c

        lax.fori_loop(0, unb_ref[unit], body, 0)

    prev = jnp.maximum(u - 1, 0)
    nxt = jnp.minimum(u + 1, _MAX_UNITS - 1)
    prev_pending = (u > 0) & (unb_ref[prev] > 0)

    @pl.when((u == 0) & (s == 0))
    def _():
        xb_ref[...] = jnp.zeros_like(xb_ref)
        load_x(u, True)

    @pl.when(prev_pending & (((nb > 0) & (s == _EXP_J)) | ((nb == 0) & (s == 0))))
    def _():
        store_y(prev, False)

    @pl.when(nb > 0)
    def _():
        @pl.when(s == 0)
        def _():
            load_x(u, False)

            def cast(r, c):
                sl = pl.ds(pl.multiple_of(r * ROW_BLOCK, ROW_BLOCK), ROW_BLOCK)
                hi, lo = _unpack_halves(xf_ref[sl, :])
                xb_ref[sl, :half] = hi
                xb_ref[sl, half:] = lo
                return c

            lax.fori_loop(0, nb, cast, 0)

        def activation(rows):
            x = xb_ref[0:rows, :]
            g = jnp.dot(x, wg_bf[...], preferred_element_type=F32) + bg_ref[...]
            up = jnp.dot(x, wu_bf[...], preferred_element_type=F32) + bu_ref[...]
            gate = jnp.minimum(g, SWIGLU_LIMIT)
            up = jnp.clip(up, -SWIGLU_LIMIT, SWIGLU_LIMIT)
            act = (up + 1.0) * gate * jax.nn.sigmoid(SWIGLU_ALPHA * gate)
            act_ref[0:rows, pl.ds(pl.multiple_of(s * tf, tf), tf)] = act.astype(BF16)

        def down(rows):
            y = jnp.dot(act_ref[0:rows, :], wd_bf[...], preferred_element_type=F32) + bd_ref[...]
            out_ref[0:rows, pl.ds(pl.multiple_of((s - _EXP_J) * tfb, tfb), tfb)] = y

        short = nb < UNIT_BLOCKS

        @pl.when(s < _EXP_J)
        def _():
            wg_bf[...] = wg_ref[...].astype(BF16)
            wu_bf[...] = wu_ref[...].astype(BF16)
            pl.when(short)(lambda: activation(UNIT_ROWS - ROW_BLOCK))
            pl.when(jnp.logical_not(short))(lambda: activation(UNIT_ROWS))

        @pl.when((s == _EXP_J) & (u + 1 < _MAX_UNITS))
        def _():
            load_x(nxt, True)

        @pl.when(s >= _EXP_J)
        def _():
            wd_bf[...] = wd_ref[...].astype(BF16)
            pl.when(short)(lambda: down(UNIT_ROWS - ROW_BLOCK))
            pl.when(jnp.logical_not(short))(lambda: down(UNIT_ROWS))

        @pl.when(s == last_step)
        def _():
            store_y(u, True)

    @pl.when((u == _MAX_UNITS - 1) & (s == last_step))
    def _():
        @pl.when(nb > 0)
        def _():
            store_y(u, False)

        out_ref[0:ROW_BLOCK, :] = jnp.zeros((ROW_BLOCK, D_MODEL), F32)
        first = utot_ref[0]
        n_blocks = y_hbm.shape[0] // ROW_BLOCK

        def start(r, c):
            _blk_copy(out_ref, 0, y_hbm, r * ROW_BLOCK, sem_out).start()
            return c

        lax.fori_loop(first, n_blocks, start, 0)

        def finish(r, c):
            _blk_copy(out_ref, 0, y_hbm, 0, sem_out).wait()
            return c

        lax.fori_loop(first, n_blocks, finish, 0)


def _experts(unit_e, unit_rb, unit_nb, unit_tot, x_rows, w_gu, b_gu3, w_down, b_down3):
    n_rows = x_rows.shape[0]
    jlast = _EXP_J - 1

    def ja(u, s, nb):
        return jnp.where(nb[u] > 0, jnp.minimum(s, jlast), jlast)

    def jb(u, s, nb):
        return jnp.where(nb[u] > 0, jnp.maximum(s - _EXP_J, 0), _EXP_JB - 1)

    tf = _EXP_TF
    tfb = _EXP_TFB
    return pl.pallas_call(
        _experts_kernel,
        grid_spec=pltpu.PrefetchScalarGridSpec(
            num_scalar_prefetch=4,
            grid=(_MAX_UNITS, _EXP_STEPS),
            in_specs=[pl.BlockSpec(memory_space=pl.ANY),
                      pl.BlockSpec((None, D_MODEL, tf), lambda u, s, e, rb, nb, tot: (e[u], 0, ja(u, s, nb))),
                      pl.BlockSpec((None, D_MODEL, tf),
                                   lambda u, s, e, rb, nb, tot: (e[u], 0, _EXP_J + ja(u, s, nb))),
                      pl.BlockSpec((None, 1, tf), lambda u, s, e, rb, nb, tot: (e[u], 0, ja(u, s, nb))),
                      pl.BlockSpec((None, 1, tf), lambda u, s, e, rb, nb, tot: (e[u], 0, _EXP_J + ja(u, s, nb))),
                      pl.BlockSpec((None, D_FF, tfb), lambda u, s, e, rb, nb, tot: (e[u], 0, jb(u, s, nb))),
                      pl.BlockSpec((None, 1, tfb), lambda u, s, e, rb, nb, tot: (e[u], 0, jb(u, s, nb)))],
            out_specs=pl.BlockSpec(memory_space=pl.ANY),
            scratch_shapes=[pltpu.VMEM((UNIT_ROWS, D_MODEL // 2), U32),
                            pltpu.VMEM((UNIT_ROWS, D_MODEL), BF16),
                            pltpu.VMEM((UNIT_ROWS, D_FF), BF16),
                            pltpu.VMEM((UNIT_ROWS, D_MODEL), F32),
                            pltpu.VMEM((D_MODEL, tf), BF16),
                            pltpu.VMEM((D_MODEL, tf), BF16),
                            pltpu.VMEM((D_FF, tfb), BF16),
                            pltpu.SemaphoreType.DMA(()),
                            pltpu.SemaphoreType.DMA(())]),
        out_shape=jax.ShapeDtypeStruct((n_rows, D_MODEL), F32),
        compiler_params=_cparams(("arbitrary", "arbitrary"), 58),
        name="experts",
    )(unit_e, unit_rb, unit_nb, unit_tot, x_rows, w_gu, w_gu, b_gu3, b_gu3, w_down, b_down3)


_CMB_TOK = 256


def _combine_kernel(dest_ref, dest_next_ref, x1_ref, comb_ref, gf_ref, y_hbm, o_ref, ybuf, sem):
    i = pl.program_id(0)
    n = pl.num_programs(0)
    slot = i % 2

    def gather(d_ref, into):
        def issue(tok, carry):
            for k in range(TOP_K):
                pltpu.make_async_copy(y_hbm.at[pl.ds(d_ref[tok * TOP_K + k], 1)],
                                      ybuf.at[into, k, pl.ds(tok, 1)], sem.at[into]).start(priority=k % 2)
            return carry

        lax.fori_loop(0, _CMB_TOK, issue, 0, unroll=4)

    @pl.when(i == 0)
    def _():
        gather(dest_ref, 0)

    @pl.when(i + 1 < n)
    def _():
        gather(dest_next_ref, 1 - slot)

    for k in range(TOP_K):
        pltpu.make_async_copy(y_hbm.at[pl.ds(0, _CMB_TOK)], ybuf.at[slot, k], sem.at[slot]).wait()

    comb = comb_ref[...]
    x2 = x1_ref[...]
    for k in range(TOP_K):
        x2 = x2 + comb[:, k:k + 1] * ybuf[slot, k]
    o_ref[...] = x2 * lax.rsqrt(jnp.mean(x2 * x2, axis=-1, keepdims=True) + RMS_EPS) * gf_ref[...]


def _combine(dest_flat, x1, comb, g_final, y_rows):
    t = x1.shape[0]
    n = t // _CMB_TOK
    row = lambda i: (i, 0)
    return pl.pallas_call(
        _combine_kernel,
        grid=(n,),
        in_specs=[pl.BlockSpec((_CMB_TOK * TOP_K,), lambda i: (i,), memory_space=pltpu.SMEM),
                  pl.BlockSpec((_CMB_TOK * TOP_K,), lambda i: (jnp.minimum(i + 1, n - 1),),
                               memory_space=pltpu.SMEM),
                  pl.BlockSpec((_CMB_TOK, D_MODEL), row),
                  pl.BlockSpec((_CMB_TOK, LANES), row),
                  pl.BlockSpec((1, D_MODEL), lambda i: (0, 0)),
                  pl.BlockSpec(memory_space=pl.ANY)],
        out_specs=pl.BlockSpec((_CMB_TOK, D_MODEL), row),
        out_shape=jax.ShapeDtypeStruct((t, D_MODEL), F32),
        scratch_shapes=[pltpu.VMEM((2, TOP_K, _CMB_TOK, D_MODEL), F32), pltpu.SemaphoreType.DMA((2,))],
        compiler_params=_cparams(("arbitrary",), 40),
        name="combine",
    )(dest_flat, dest_flat, x1, comb, g_final, y_rows)


def kernel(x, positions, norm_mix_g, w_in, ret_w_o, moba_w_o, w_out, norm_ffn_g, router_w, router_b,
           exp_w_gu, exp_b_gu, exp_w_down, exp_b_down, norm_final_g):
    batch, seq, d = x.shape
    t = batch * seq
    depth = w_in.shape[0]
    assert depth == 1, "the combine kernel applies the closing norm, so exactly one layer is supported"
    half = MOBA_HEAD_DIM // 2
    inv_freq = ROPE_THETA ** (-jnp.arange(half, dtype=F32) / half)
    inv_full = jnp.concatenate([inv_freq, inv_freq])[None, :]
    log_gamma = jnp.log1p(-jnp.exp2(-5.0 - jnp.arange(RET_HEADS, dtype=F32)))
    cos_t, sin_t = _rope_tables(positions.reshape(t, 1), inv_full)
    n_rows = (-(-(t * TOP_K) // ROW_BLOCK) + N_EXPERTS) * ROW_BLOCK

    x2 = x.reshape(t, d)
    for l in range(depth):
        proj = _in_proj(x2, norm_mix_g[l][None, :], w_in[l].astype(BF16), cos_t, sin_t)
        o_ret = _retention(log_gamma, proj, batch, seq)
        o_moba = _moba(proj, batch, seq)
        rw_pad = jnp.pad(router_w[l], ((0, 0), (0, LANES - N_EXPERTS)))
        rb_pad = jnp.pad(router_b[l], (0, LANES - N_EXPERTS), constant_values=NEG_INF)[None, :]
        x1, h2, logits = _out_proj(o_ret, o_moba, proj, x2, ret_w_o[l].astype(BF16), moba_w_o[l].astype(BF16),
                                   w_out[l].astype(BF16), norm_ffn_g[l][None, :], rw_pad, rb_pad)
        dest, comb, units = _routing(logits)
        dest_flat = dest[:, :TOP_K].reshape(t * TOP_K)
        x_rows = _dispatch(dest_flat, h2, n_rows)
        y_rows = _experts(units[:, 0], units[:, 1], units[:, 2], units[0:1, 3], x_rows, exp_w_gu[l],
                          exp_b_gu[l][:, None, :], exp_w_down[l], exp_b_down[l][:, None, :])
        x2 = _combine(dest_flat, x1, comb, norm_final_g[None, :], y_rows)
    return x2.reshape(batch, seq, d)
```
